```python
import math
import jax, jax.numpy as jnp
from jax import lax
import numpy as np

D_MODEL = 1024
BATCH = 16
SEQ = 4096
DEPTH = 1

EXPAND = 2
D_INNER = EXPAND * D_MODEL
D_SB = D_INNER // 2
D_SSD = D_INNER - D_SB
SB_HEAD_DIM = 64
SB_HEADS = D_SB // SB_HEAD_DIM
SB_BLOCK = 128
SSD_HEAD_DIM = 64
SSD_HEADS = D_SSD // SSD_HEAD_DIM
SSD_GROUPS = 2
SSD_STATE = 128
SSD_CHUNK = 128
CONV_W = 4
CONV_DIM = D_SSD + 2 * SSD_GROUPS * SSD_STATE
D_IN_PROJ = 4 * D_SB + D_SSD + CONV_DIM + SSD_HEADS
EPS = 1e-6

kernel_name = "hymba_stickbreaking_ssd_layer"


def rmsnorm(x, w):
    xf = x.astype(jnp.float32)
    y = xf * lax.rsqrt(jnp.mean(xf * xf, axis=-1, keepdims=True) + EPS)
    return (y * w.astype(jnp.float32)).astype(x.dtype)


def causal_depthwise_conv(u, w, b):
    l = u.shape[1]
    up = jnp.pad(u, ((0, 0), (CONV_W - 1, 0), (0, 0)))
    out = b
    for i in range(CONV_W):
        out = out + up[:, i:i + l] * w[i]
    return out


def stick_breaking_attention(q, k, v):
    b, l, h, d = q.shape
    nb = l // SB_BLOCK
    scale = 1.0 / math.sqrt(d)
    qb = q.astype(jnp.float32).reshape(b, nb, SB_BLOCK, h, d).transpose(1, 0, 2, 3, 4)
    kf = k.astype(jnp.float32)
    vf = v.astype(jnp.float32)
    key_pos = jnp.arange(l)

    def block(args):
        qi, i = args
        z = jnp.einsum('bqhd,bkhd->bhqk', qi, kf) * scale
        q_pos = i * SB_BLOCK + jnp.arange(SB_BLOCK)
        mask = key_pos[None, :] < q_pos[:, None]
        log_keep = jnp.where(mask, jax.nn.log_sigmoid(-z), 0.0)
        rest = lax.cumsum(log_keep, axis=3, reverse=True) - log_keep
        w = jnp.where(mask, jnp.exp(jax.nn.log_sigmoid(z) + rest), 0.0)
        return jnp.einsum('bhqk,bkhd->bqhd', w, vf)

    o = lax.map(block, (qb, jnp.arange(nb)))
    return o.transpose(1, 0, 2, 3, 4).reshape(b, l, h, d)


def ssd_chunked(xh, dt, a, Bm, Cm):
    b, l, h, p = xh.shape
    g, n = Bm.shape[2], Bm.shape[3]
    hpg = h // g
    c = l // SSD_CHUNK
    Q = SSD_CHUNK
    x = (xh.astype(jnp.float32) * dt[..., None]).reshape(b, c, Q, g, hpg, p)
    dA = (dt * a).reshape(b, c, Q, g, hpg).transpose(0, 1, 3, 4, 2)
    Bc = Bm.astype(jnp.float32).reshape(b, c, Q, g, n)
    Cc = Cm.astype(jnp.float32).reshape(b, c, Q, g, n)
    a_cs = jnp.cumsum(dA, axis=-1)

    seg = a_cs[..., :, None] - a_cs[..., None, :]
    tri = jnp.tril(jnp.ones((Q, Q), dtype=bool))
    Lmat = jnp.exp(jnp.where(tri, seg, -jnp.inf))
    cb = jnp.einsum('bctgn,bcsgn->bcgts', Cc, Bc)
    wts = cb[:, :, :, None] * Lmat
    y_diag = jnp.einsum('bcgkts,bcsgkp->bctgkp', wts, x)

    decay_to_end = jnp.exp(a_cs[..., -1:] - a_cs)
    states = jnp.einsum('bcsgn,bcgks,bcsgkp->bcgkpn', Bc, decay_to_end, x)
    chunk_decay = jnp.exp(a_cs[..., -1])

    def step(carry, inp):
        st, dec = inp
        return carry * dec[..., None, None] + st, carry

    init = jnp.zeros((b, g, hpg, p, n), jnp.float32)
    _, states_in = lax.scan(step, init, (jnp.moveaxis(states, 1, 0), jnp.moveaxis(chunk_decay, 1, 0)))
    states_in = jnp.moveaxis(states_in, 0, 1)

    y_off = jnp.einsum('bctgn,bcgkpn,bcgkt->bctgkp', Cc, states_in, jnp.exp(a_cs))
    return (y_diag + y_off).reshape(b, l, h, p)


def hybrid_layer(x, norm_w, w_in, q_norm_w, k_norm_w, conv_w, conv_b, dt_bias,
                 A_log, D_skip, sb_norm_w, ssd_norm_w, w_out):
    b, l, _ = x.shape
    hn = rmsnorm(x, norm_w)
    proj = hn @ w_in
    i1 = D_SB; i2 = 2 * D_SB; i3 = 3 * D_SB; i4 = 4 * D_SB
    i5 = i4 + D_SSD; i6 = i5 + CONV_DIM
    q, k, v, z_sb, z_ssd, xBC, dt_raw = jnp.split(proj, [i1, i2, i3, i4, i5, i6], axis=-1)

    q = rmsnorm(q.reshape(b, l, SB_HEADS, SB_HEAD_DIM), q_norm_w)
    k = rmsnorm(k.reshape(b, l, SB_HEADS, SB_HEAD_DIM), k_norm_w)
    v = v.reshape(b, l, SB_HEADS, SB_HEAD_DIM)
    o_sb = stick_breaking_attention(q, k, v).reshape(b, l, D_SB).astype(x.dtype)
    y_sb = rmsnorm(o_sb * jax.nn.silu(z_sb), sb_norm_w)

    xBC = jax.nn.silu(causal_depthwise_conv(xBC, conv_w, conv_b))
    xs, Bm, Cm = jnp.split(xBC, [D_SSD, D_SSD + SSD_GROUPS * SSD_STATE], axis=-1)
    xs = xs.reshape(b, l, SSD_HEADS, SSD_HEAD_DIM)
    dt = jax.nn.softplus(dt_raw.astype(jnp.float32) + dt_bias.astype(jnp.float32))
    a = -jnp.exp(A_log.astype(jnp.float32))
    y = ssd_chunked(xs, dt, a,
                    Bm.reshape(b, l, SSD_GROUPS, SSD_STATE),
                    Cm.reshape(b, l, SSD_GROUPS, SSD_STATE))
    y = y + D_skip.astype(jnp.float32)[:, None] * xs.astype(jnp.float32)
    y = y.reshape(b, l, D_SSD).astype(x.dtype)
    y_ssd = rmsnorm(y * jax.nn.silu(z_ssd), ssd_norm_w)

    mixed = jnp.concatenate([y_sb, y_ssd], axis=-1)
    return x + mixed @ w_out


def setup_inputs(seed: int = 0) -> dict:
    key = jax.random.key(seed)
    ks = jax.random.split(key, 14)
    f32 = jnp.float32
    x = jax.random.normal(ks[0], (BATCH, SEQ, D_MODEL), f32)
    norm_w = 1.0 + 0.02 * jax.random.normal(ks[1], (DEPTH, D_MODEL), f32)
    w_in = jax.random.normal(ks[2], (DEPTH, D_MODEL, D_IN_PROJ), f32) * D_MODEL ** -0.5
    q_norm_w = 1.0 + 0.02 * jax.random.normal(ks[3], (DEPTH, SB_HEAD_DIM), f32)
    k_norm_w = 1.0 + 0.02 * jax.random.normal(ks[4], (DEPTH, SB_HEAD_DIM), f32)
    conv_w = jax.random.normal(ks[5], (DEPTH, CONV_W, CONV_DIM), f32) * CONV_W ** -0.5
    conv_b = 0.01 * jax.random.normal(ks[6], (DEPTH, CONV_DIM), f32)
    u = jax.random.uniform(ks[7], (DEPTH, SSD_HEADS), f32)
    dt0 = jnp.exp(u * (math.log(0.1) - math.log(0.001)) + math.log(0.001))
    dt_bias = dt0 + jnp.log(-jnp.expm1(-dt0))
    A_log = jnp.log(jax.random.uniform(ks[8], (DEPTH, SSD_HEADS), f32, minval=1.0, maxval=16.0))
    D_skip = 1.0 + 0.02 * jax.random.normal(ks[9], (DEPTH, SSD_HEADS), f32)
    sb_norm_w = 1.0 + 0.02 * jax.random.normal(ks[10], (DEPTH, D_SB), f32)
    ssd_norm_w = 1.0 + 0.02 * jax.random.normal(ks[11], (DEPTH, D_SSD), f32)
    w_out = jax.random.normal(ks[12], (DEPTH, D_INNER, D_MODEL), f32) * D_INNER ** -0.5
    return {"x": x, "norm_w": norm_w, "w_in": w_in, "q_norm_w": q_norm_w,
            "k_norm_w": k_norm_w, "conv_w": conv_w, "conv_b": conv_b,
            "dt_bias": dt_bias, "A_log": A_log, "D_skip": D_skip,
            "sb_norm_w": sb_norm_w, "ssd_norm_w": ssd_norm_w, "w_out": w_out}


def reference(x, norm_w, w_in, q_norm_w, k_norm_w, conv_w, conv_b, dt_bias,
              A_log, D_skip, sb_norm_w, ssd_norm_w, w_out):
    for layer in range(DEPTH):
        x = hybrid_layer(x, norm_w[layer], w_in[layer], q_norm_w[layer], k_norm_w[layer],
                         conv_w[layer], conv_b[layer], dt_bias[layer], A_log[layer],
                         D_skip[layer], sb_norm_w[layer], ssd_norm_w[layer], w_out[layer])
    return x
```

```python
import functools
import math

import jax
import jax.numpy as jnp
from jax import lax
from jax.experimental import pallas as pl
from jax.experimental.pallas import tpu as pltpu

F32 = jnp.float32
BF16 = jnp.bfloat16

EPS = 1e-6
LANES = 128
HEAD_DIM = 64
SSD_STATE = 128
SSD_GROUPS = 2
SSD_CHUNK = 128
CONV_W = 4
CONV_HALO = 8

ROW_TILE = 512
ATT_BLOCK = 256
VMEM_LIMIT = 56 * 1024 * 1024


def _silu(z):
    return z * (1.0 / (1.0 + jnp.exp(-z)))


def _softplus(z):
    return jnp.maximum(z, 0.0) + jnp.log(1.0 + jnp.exp(-jnp.abs(z)))


def _in_proj_kernel(x_ref, nw_ref, w_ref, wdt_ref,
                    q_ref, k_ref, v_ref, zsb_ref, zssd_ref, xbc_ref, dt_ref):
    x = x_ref[...]
    ms = jnp.mean(x * x, axis=-1, keepdims=True)
    hn = (x * lax.rsqrt(ms + EPS) * nw_ref[...]).astype(BF16)
    start = 0
    for ref in (q_ref, k_ref, v_ref, zsb_ref, zssd_ref, xbc_ref):
        width = ref.shape[-1]
        ref[...] = jnp.dot(hn, w_ref[:, start:start + width],
                           preferred_element_type=F32).astype(ref.dtype)
        start += width
    dt_ref[...] = jnp.dot(hn, wdt_ref[...], preferred_element_type=F32)


def _in_proj(x2, norm_w, w_main, w_dt, d_sb, d_ssd, conv_dim):
    m, d = x2.shape
    widths = (d_sb, d_sb, d_sb, d_sb, d_ssd, conv_dim)
    row = lambda i: (i, 0)
    fixed = lambda i: (0, 0)
    out_shape = [jax.ShapeDtypeStruct((m, w), BF16) for w in widths]
    out_shape.append(jax.ShapeDtypeStruct((m, LANES), F32))
    out_specs = [pl.BlockSpec((ROW_TILE, w), row) for w in widths]
    out_specs.append(pl.BlockSpec((ROW_TILE, LANES), row))
    return pl.pallas_call(
        _in_proj_kernel,
        grid=(m // ROW_TILE,),
        in_specs=[
            pl.BlockSpec((ROW_TILE, d), row),
            pl.BlockSpec((1, d), fixed),
            pl.BlockSpec(w_main.shape, fixed, pipeline_mode=pl.Buffered(1)),
            pl.BlockSpec(w_dt.shape, fixed, pipeline_mode=pl.Buffered(1)),
        ],
        out_specs=out_specs,
        out_shape=out_shape,
        compiler_params=pltpu.CompilerParams(
            dimension_semantics=("arbitrary",), vmem_limit_bytes=VMEM_LIMIT),
        name="in_proj",
    )(x2, norm_w.reshape(1, d), w_main, w_dt)


def _attn_kernel(q_ref, k_ref, v_ref, qnw_ref, knw_ref, u_ref, o_ref, kn_ref, *, seq_len):
    qi = pl.program_id(2)
    blk = ATT_BLOCK
    first = lax.broadcasted_iota(jnp.int32, (1, LANES), 1) < HEAD_DIM

    def head_norm(t, w):
        sq = t * t
        s0 = jnp.sum(jnp.where(first, sq, 0.0), axis=1, keepdims=True)
        s1 = jnp.sum(jnp.where(first, 0.0, sq), axis=1, keepdims=True)
        ms = jnp.where(first, s0, s1) * (1.0 / HEAD_DIM)
        return t * lax.rsqrt(ms + EPS) * w

    @pl.when(qi == 0)
    def _():
        rows = 512

        def body(i, carry):
            r = pl.multiple_of(i * rows, rows)
            kk = k_ref[0, pl.ds(r, rows), :].astype(F32)
            kn_ref[pl.ds(r, rows), :] = head_norm(kk, knw_ref[...]).astype(BF16)
            return carry

        lax.fori_loop(0, seq_len // rows, body, 0)

    qn = head_norm(q_ref[0].astype(F32), qnw_ref[...]) * (1.0 / math.sqrt(HEAD_DIM))
    r_id = lax.broadcasted_iota(jnp.int32, (blk, blk), 0)
    c_id = lax.broadcasted_iota(jnp.int32, (blk, blk), 1)
    causal = c_id < r_id

    outs = []
    for h in range(2):
        qm = jnp.where(first if h == 0 else jnp.logical_not(first), qn, 0.0).astype(BF16)

        def block(j, acc, c, masked, qm=qm):
            off = pl.multiple_of(j * blk, blk)
            kb = kn_ref[pl.ds(off, blk), :]
            vb = v_ref[0, pl.ds(off, blk), :]
            z = lax.dot_general(qm, kb, (((1,), (1,)), ((), ())), preferred_element_type=F32)
            sp = _softplus(z)
            lq = z - sp
            if masked:
                sp = jnp.where(causal, sp, 0.0)
            rl = jnp.dot(sp.astype(BF16), u_ref[...], preferred_element_type=F32)
            w = jnp.exp(lq + rl)
            if masked:
                w = jnp.where(causal, w, 0.0)
            pv = jnp.dot(w.astype(BF16), vb, preferred_element_type=F32)
            acc = acc + jnp.exp(c) * pv
            c = c - jnp.sum(sp, axis=1, keepdims=True)
            return acc, c

        acc, c = block(qi, jnp.zeros((blk, LANES), F32), jnp.zeros((blk, 1), F32), True)

        def body(i, carry, block=block):
            return block(qi - 1 - i, carry[0], carry[1], False)

        acc, c = lax.fori_loop(0, qi, body, (acc, c))
        outs.append(acc)
    o_ref[0] = jnp.where(first, outs[0], outs[1]).astype(o_ref.dtype)


def _sb_attention(q, k, v, q_norm_w, k_norm_w):
    b, l, d = q.shape
    blk = ATT_BLOCK
    pairs = d // LANES
    r_id = lax.broadcasted_iota(jnp.int32, (blk, blk), 0)
    c_id = lax.broadcasted_iota(jnp.int32, (blk, blk), 1)
    u_neg = jnp.where(r_id > c_id, -1.0, 0.0).astype(BF16)
    qnw = jnp.tile(q_norm_w.astype(F32), LANES // HEAD_DIM).reshape(1, LANES)
    knw = jnp.tile(k_norm_w.astype(F32), LANES // HEAD_DIM).reshape(1, LANES)
    fixed = lambda bi, p, qi: (0, 0)
    return pl.pallas_call(
        functools.partial(_attn_kernel, seq_len=l),
        grid=(b, pairs, l // blk),
        in_specs=[
            pl.BlockSpec((1, blk, LANES), lambda bi, p, qi: (bi, qi, p)),
            pl.BlockSpec((1, l, LANES), lambda bi, p, qi: (bi, 0, p)),
            pl.BlockSpec((1, l, LANES), lambda bi, p, qi: (bi, 0, p)),
            pl.BlockSpec((1, LANES), fixed),
            pl.BlockSpec((1, LANES), fixed),
            pl.BlockSpec((blk, blk), fixed),
        ],
        out_specs=pl.BlockSpec((1, blk, LANES), lambda bi, p, qi: (bi, qi, p)),
        out_shape=jax.ShapeDtypeStruct((b, l, d), BF16),
        scratch_shapes=[pltpu.VMEM((l, LANES), BF16)],
        compiler_params=pltpu.CompilerParams(
            dimension_semantics=("arbitrary", "arbitrary", "arbitrary"),
            vmem_limit_bytes=VMEM_LIMIT),
        name="sb_attn",
    )(q, k, v, qnw, knw, u_neg)


def _ssd_kernel(xbc_ref, dt_ref, z_ref, cw_ref, cb_ref, dtb_ref, alog_ref, dskip_ref, nw_ref,
                expand_ref, y_ref, ext_ref, state_ref, *, d_ssd):
    ci = pl.program_id(1)
    q = SSD_CHUNK
    gw = SSD_STATE
    heads_per_group = d_ssd // HEAD_DIM // SSD_GROUPS
    group_w = heads_per_group * HEAD_DIM

    @pl.when(ci == 0)
    def _():
        ext_ref[0:CONV_HALO, :] = jnp.zeros((CONV_HALO, ext_ref.shape[1]), F32)
        state_ref[...] = jnp.zeros(state_ref.shape, F32)

    @pl.when(ci > 0)
    def _():
        ext_ref[0:CONV_HALO, :] = ext_ref[q:q + CONV_HALO, :]

    ext_ref[CONV_HALO:CONV_HALO + q, :] = xbc_ref[0].astype(F32)

    conv = cb_ref[...]
    for i in range(CONV_W):
        s = CONV_HALO - (CONV_W - 1) + i
        conv = conv + ext_ref[s:s + q, :] * cw_ref[i:i + 1, :]
    u = _silu(conv)
    xs = u[:, :d_ssd]
    b_all = u[:, d_ssd:d_ssd + SSD_GROUPS * gw]
    c_all = u[:, d_ssd + SSD_GROUPS * gw:]

    dt = _softplus(dt_ref[0] + dtb_ref[...])
    d_a = dt * (-jnp.exp(alog_ref[...]))
    r_id = lax.broadcasted_iota(jnp.int32, (q, q), 0)
    c_id = lax.broadcasted_iota(jnp.int32, (q, q), 1)
    lower = c_id <= r_id
    tri = jnp.where(lower, 1.0, 0.0).astype(BF16)
    d_a_hi = d_a.astype(BF16)
    d_a_lo = (d_a - d_a_hi.astype(F32)).astype(BF16)
    a_cs = (jnp.dot(tri, d_a_hi, preferred_element_type=F32)
            + jnp.dot(tri, d_a_lo, preferred_element_type=F32))
    a_cs_t = a_cs.T
    a_last = a_cs[q - 1:q, :]

    def expand(t):
        return jnp.dot(t.astype(BF16), expand_ref[...], preferred_element_type=F32)

    dt_e = expand(dt)
    fac_e = expand(dt * jnp.exp(a_last - a_cs))
    ea_e = expand(jnp.exp(a_cs))

    xdt = xs * dt_e
    xfac = (xs * fac_e).astype(BF16)
    first = lax.broadcasted_iota(jnp.int32, (1, LANES), 1) < HEAD_DIM

    y_parts = []
    new_states = []
    for g in range(SSD_GROUPS):
        bg = b_all[:, g * gw:(g + 1) * gw]
        cg = c_all[:, g * gw:(g + 1) * gw].astype(BF16)
        cb = lax.dot_general(cg, bg.astype(BF16), (((1,), (1,)), ((), ())),
                             preferred_element_type=F32)
        bg_t = bg.T.astype(BF16)
        lo, hi = g * group_w, (g + 1) * group_w
        st_old = state_ref[:, lo:hi]
        y_off = jnp.dot(cg, st_old.astype(BF16), preferred_element_type=F32) * ea_e[:, lo:hi]
        new_states.append(st_old * ea_e[q - 1:q, lo:hi]
                          + jnp.dot(bg_t, xfac[:, lo:hi], preferred_element_type=F32))
        for m in range(heads_per_group // 2):
            base = lo + m * LANES
            x_pair = xdt[:, base:base + LANES]
            y_pair = None
            for h in range(2):
                kh = base // HEAD_DIM + h
                seg = a_cs[:, kh:kh + 1] - a_cs_t[kh:kh + 1, :]
                decay = jnp.where(lower, jnp.exp(jnp.minimum(seg, 0.0)), 0.0)
                wts = (cb * decay).astype(BF16)
                x_h = jnp.where(first if h == 0 else jnp.logical_not(first), x_pair, 0.0)
                part = jnp.dot(wts, x_h.astype(BF16), preferred_element_type=F32)
                y_pair = part if y_pair is None else y_pair + part
            y_parts.append(y_pair + y_off[:, m * LANES:(m + 1) * LANES])
    for g in range(SSD_GROUPS):
        state_ref[:, g * group_w:(g + 1) * group_w] = new_states[g]

    y = jnp.concatenate(y_parts, axis=1) + dskip_ref[...] * xs
    gated = y * _silu(z_ref[0].astype(F32))
    ms = jnp.mean(gated * gated, axis=-1, keepdims=True)
    y_ref[0] = (gated * lax.rsqrt(ms + EPS) * nw_ref[...]).astype(y_ref.dtype)


def _ssd(xbc, dt_raw, z_ssd, conv_w, conv_b, dt_bias, a_log, d_skip, ssd_norm_w, d_ssd):
    b, l, conv_dim = xbc.shape
    heads = d_ssd // HEAD_DIM
    q = SSD_CHUNK
    pad = LANES - heads
    dtb = jnp.pad(dt_bias.astype(F32), (0, pad)).reshape(1, LANES)
    alog = jnp.pad(a_log.astype(F32), (0, pad)).reshape(1, LANES)
    dskip_e = jnp.repeat(d_skip.astype(F32), HEAD_DIM).reshape(1, d_ssd)
    lane_head = lax.broadcasted_iota(jnp.int32, (LANES, d_ssd), 1) // HEAD_DIM
    row_head = lax.broadcasted_iota(jnp.int32, (LANES, d_ssd), 0)
    expand = jnp.where(lane_head == row_head, 1.0, 0.0).astype(BF16)
    chunk = lambda bi, ci: (bi, ci, 0)
    fixed = lambda bi, ci: (0, 0)
    return pl.pallas_call(
        functools.partial(_ssd_kernel, d_ssd=d_ssd),
        grid=(b, l // q),
        in_specs=[
            pl.BlockSpec((1, q, conv_dim), chunk),
            pl.BlockSpec((1, q, LANES), chunk),
            pl.BlockSpec((1, q, d_ssd), chunk),
            pl.BlockSpec((CONV_W, conv_dim), fixed),
            pl.BlockSpec((1, conv_dim), fixed),
            pl.BlockSpec((1, LANES), fixed),
            pl.BlockSpec((1, LANES), fixed),
            pl.BlockSpec((1, d_ssd), fixed),
            pl.BlockSpec((1, d_ssd), fixed),
            pl.BlockSpec((LANES, d_ssd), fixed),
        ],
        out_specs=pl.BlockSpec((1, q, d_ssd), chunk),
        out_shape=jax.ShapeDtypeStruct((b, l, d_ssd), BF16),
        scratch_shapes=[pltpu.VMEM((q + CONV_HALO, conv_dim), F32),
                        pltpu.VMEM((SSD_STATE, d_ssd), F32)],
        compiler_params=pltpu.CompilerParams(
            dimension_semantics=("arbitrary", "arbitrary"), vmem_limit_bytes=VMEM_LIMIT),
        name="ssd",
    )(xbc, dt_raw, z_ssd, conv_w.astype(F32), conv_b.astype(F32).reshape(1, conv_dim),
      dtb, alog, dskip_e, ssd_norm_w.astype(F32).reshape(1, d_ssd), expand)


def _out_proj_kernel(o_ref, z_ref, y_ref, x_ref, nw_ref, w_ref, out_ref):
    d_sb = o_ref.shape[-1]
    gated = o_ref[...].astype(F32) * _silu(z_ref[...].astype(F32))
    ms = jnp.mean(gated * gated, axis=-1, keepdims=True)
    y_sb = (gated * lax.rsqrt(ms + EPS) * nw_ref[...]).astype(BF16)
    out_ref[...] = (x_ref[...]
                    + jnp.dot(y_sb, w_ref[:d_sb, :], preferred_element_type=F32)
                    + jnp.dot(y_ref[...], w_ref[d_sb:, :], preferred_element_type=F32))


def _out_proj(o_sb, z_sb, y_ssd, x2, sb_norm_w, w_out):
    m, d = x2.shape
    d_sb = o_sb.shape[-1]
    d_ssd = y_ssd.shape[-1]
    row = lambda i: (i, 0)
    fixed = lambda i: (0, 0)
    return pl.pallas_call(
        _out_proj_kernel,
        grid=(m // ROW_TILE,),
        in_specs=[
            pl.BlockSpec((ROW_TILE, d_sb), row),
            pl.BlockSpec((ROW_TILE, d_sb), row),
            pl.BlockSpec((ROW_TILE, d_ssd), row),
            pl.BlockSpec((ROW_TILE, d), row),
            pl.BlockSpec((1, d_sb), fixed),
            pl.BlockSpec(w_out.shape, fixed, pipeline_mode=pl.Buffered(1)),
        ],
        out_specs=pl.BlockSpec((ROW_TILE, d), row),
        out_shape=jax.ShapeDtypeStruct((m, d), F32),
        compiler_params=pltpu.CompilerParams(
            dimension_semantics=("arbitrary",), vmem_limit_bytes=VMEM_LIMIT),
        name="out_proj",
    )(o_sb, z_sb, y_ssd, x2, sb_norm_w.astype(F32).reshape(1, d_sb), w_out)


def _layer(x, norm_w, w_in, q_norm_w, k_norm_w, conv_w, conv_b, dt_bias, a_log, d_skip,
           sb_norm_w, ssd_norm_w, w_out):
    b, l, d = x.shape
    d_sb = sb_norm_w.shape[0]
    d_ssd = ssd_norm_w.shape[0]
    conv_dim = conv_w.shape[1]
    heads = dt_bias.shape[0]
    main = 4 * d_sb + d_ssd + conv_dim
    assert w_in.shape[1] == main + heads and heads <= LANES
    assert conv_dim == d_ssd + 2 * SSD_GROUPS * SSD_STATE and heads * HEAD_DIM == d_ssd
    m = b * l
    x2 = x.reshape(m, d)
    w_main = w_in[:, :main].astype(BF16)
    w_dt = jnp.pad(w_in[:, main:], ((0, 0), (0, LANES - heads))).astype(BF16)
    q, k, v, z_sb, z_ssd, xbc, dt_raw = _in_proj(
        x2, norm_w.astype(F32), w_main, w_dt, d_sb, d_ssd, conv_dim)
    o_sb = _sb_attention(q.reshape(b, l, d_sb), k.reshape(b, l, d_sb), v.reshape(b, l, d_sb),
                         q_norm_w, k_norm_w)
    y_ssd = _ssd(xbc.reshape(b, l, conv_dim), dt_raw.reshape(b, l, LANES),
                 z_ssd.reshape(b, l, d_ssd), conv_w, conv_b, dt_bias, a_log, d_skip,
                 ssd_norm_w, d_ssd)
    out = _out_proj(o_sb.reshape(m, d_sb), z_sb, y_ssd.reshape(m, d_ssd), x2, sb_norm_w,
                    w_out.astype(BF16))
    return out.reshape(b, l, d)


def kernel(x, norm_w, w_in, q_norm_w, k_norm_w, conv_w, conv_b, dt_bias, A_log, D_skip,
           sb_norm_w, ssd_norm_w, w_out):
    for layer in range(norm_w.shape[0]):
        x = _layer(x, norm_w[layer], w_in[layer], q_norm_w[layer], k_norm_w[layer],
                   conv_w[layer], conv_b[layer], dt_bias[layer], A_log[layer], D_skip[layer],
                   sb_norm_w[layer], ssd_norm_w[layer], w_out[layer])
    return x
```

```python
import functools
import math

import jax
import jax.numpy as jnp
from jax import lax
from jax.experimental import pallas as pl
from jax.experimental.pallas import tpu as pltpu

F32 = jnp.float32
BF16 = jnp.bfloat16

EPS = 1e-6
LANES = 128
HEAD_DIM = 64
SSD_STATE = 128
SSD_GROUPS = 2
SSD_CHUNK = 128
CONV_W = 4
CONV_HALO = 8

ROW_TILE = 512
ATT_BLOCK = 256
VMEM_LIMIT = 56 * 1024 * 1024
LOG_STICK_FLOOR = -104.0


def _silu(z):
    return z * (1.0 / (1.0 + jnp.exp(-z)))


def _softplus(z):
    return jnp.maximum(z, 0.0) + jnp.log(1.0 + jnp.exp(-jnp.abs(z)))


def _in_proj_kernel(x_ref, nw_ref, w_ref, wdt_ref,
                    q_ref, k_ref, v_ref, zsb_ref, zssd_ref, xbc_ref, dt_ref):
    x = x_ref[...]
    ms = jnp.mean(x * x, axis=-1, keepdims=True)
    hn = (x * lax.rsqrt(ms + EPS) * nw_ref[...]).astype(BF16)
    start = 0
    for ref in (q_ref, k_ref, v_ref, zsb_ref, zssd_ref, xbc_ref):
        width = ref.shape[-1]
        ref[...] = jnp.dot(hn, w_ref[:, start:start + width],
                           preferred_element_type=F32).astype(ref.dtype)
        start += width
    dt_ref[...] = jnp.dot(hn, wdt_ref[...], preferred_element_type=F32)


def _in_proj(x2, norm_w, w_main, w_dt, d_sb, d_ssd, conv_dim):
    m, d = x2.shape
    widths = (d_sb, d_sb, d_sb, d_sb, d_ssd, conv_dim)
    row = lambda i: (i, 0)
    fixed = lambda i: (0, 0)
    out_shape = [jax.ShapeDtypeStruct((m, w), BF16) for w in widths]
    out_shape.append(jax.ShapeDtypeStruct((m, LANES), F32))
    out_specs = [pl.BlockSpec((ROW_TILE, w), row) for w in widths]
    out_specs.append(pl.BlockSpec((ROW_TILE, LANES), row))
    return pl.pallas_call(
        _in_proj_kernel,
        grid=(m // ROW_TILE,),
        in_specs=[
            pl.BlockSpec((ROW_TILE, d), row),
            pl.BlockSpec((1, d), fixed),
            pl.BlockSpec(w_main.shape, fixed, pipeline_mode=pl.Buffered(1)),
            pl.BlockSpec(w_dt.shape, fixed, pipeline_mode=pl.Buffered(1)),
        ],
        out_specs=out_specs,
        out_shape=out_shape,
        compiler_params=pltpu.CompilerParams(
            dimension_semantics=("arbitrary",), vmem_limit_bytes=VMEM_LIMIT),
        name="in_proj",
    )(x2, norm_w.reshape(1, d), w_main, w_dt)


def _attn_kernel(q_ref, k_ref, v_ref, qnw_ref, knw_ref, u_ref, o_ref, kn_ref, *, seq_len):
    qi = pl.program_id(2)
    blk = ATT_BLOCK
    first = lax.broadcasted_iota(jnp.int32, (1, LANES), 1) < HEAD_DIM

    def head_norm(t, w):
        sq = t * t
        s0 = jnp.sum(jnp.where(first, sq, 0.0), axis=1, keepdims=True)
        s1 = jnp.sum(jnp.where(first, 0.0, sq), axis=1, keepdims=True)
        ms = jnp.where(first, s0, s1) * (1.0 / HEAD_DIM)
        return t * lax.rsqrt(ms + EPS) * w

    @pl.when(qi == 0)
    def _():
        rows = 512

        def body(i, carry):
            r = pl.multiple_of(i * rows, rows)
            kk = k_ref[0, pl.ds(r, rows), :].astype(F32)
            kn_ref[pl.ds(r, rows), :] = head_norm(kk, knw_ref[...]).astype(BF16)
            return carry

        lax.fori_loop(0, seq_len // rows, body, 0)

    qn = head_norm(q_ref[0].astype(F32), qnw_ref[...]) * (1.0 / math.sqrt(HEAD_DIM))
    r_id = lax.broadcasted_iota(jnp.int32, (blk, blk), 0)
    c_id = lax.broadcasted_iota(jnp.int32, (blk, blk), 1)
    causal = c_id < r_id

    qms = [jnp.where(first if h == 0 else jnp.logical_not(first), qn, 0.0).astype(BF16)
           for h in range(2)]

    def sweep(j, accs, cs, masked):
        off = pl.multiple_of(j * blk, blk)
        kb = kn_ref[pl.ds(off, blk), :]
        vb = v_ref[0, pl.ds(off, blk), :]
        new_accs, new_cs = [], []
        for h in range(2):
            z = lax.dot_general(qms[h], kb, (((1,), (1,)), ((), ())),
                                preferred_element_type=F32)
            sp = _softplus(z)
            lq = z - sp
            if masked:
                sp = jnp.where(causal, sp, 0.0)
            rl = jnp.dot(sp.astype(BF16), u_ref[...], preferred_element_type=F32)
            w = jnp.exp(lq + rl)
            if masked:
                w = jnp.where(causal, w, 0.0)
            pv = jnp.dot(w.astype(BF16), vb, preferred_element_type=F32)
            new_accs.append(accs[h] + jnp.exp(cs[h]) * pv)
            new_cs.append(cs[h] - jnp.sum(sp, axis=1, keepdims=True))
        return new_accs, new_cs

    def log_stick_left(cs):
        return jnp.max(jnp.maximum(cs[0], cs[1]))

    accs, cs = sweep(qi, [jnp.zeros((blk, LANES), F32)] * 2, [jnp.zeros((blk, 1), F32)] * 2, True)

    def cond(carry):
        return jnp.logical_and(carry[0] >= 0, carry[1] > LOG_STICK_FLOOR)

    def body(carry):
        j, _, a0, a1, c0, c1 = carry
        accs, cs = sweep(j, [a0, a1], [c0, c1], False)
        return (j - 1, log_stick_left(cs), accs[0], accs[1], cs[0], cs[1])

    carry = lax.while_loop(cond, body,
                           (qi - 1, log_stick_left(cs), accs[0], accs[1], cs[0], cs[1]))
    o_ref[0] = jnp.where(first, carry[2], carry[3]).astype(o_ref.dtype)


def _sb_attention(q, k, v, q_norm_w, k_norm_w):
    b, l, d = q.shape
    blk = ATT_BLOCK
    pairs = d // LANES
    r_id = lax.broadcasted_iota(jnp.int32, (blk, blk), 0)
    c_id = lax.broadcasted_iota(jnp.int32, (blk, blk), 1)
    u_neg = jnp.where(r_id > c_id, -1.0, 0.0).astype(BF16)
    qnw = jnp.tile(q_norm_w.astype(F32), LANES // HEAD_DIM).reshape(1, LANES)
    knw = jnp.tile(k_norm_w.astype(F32), LANES // HEAD_DIM).reshape(1, LANES)
    fixed = lambda bi, p, qi: (0, 0)
    return pl.pallas_call(
        functools.partial(_attn_kernel, seq_len=l),
        grid=(b, pairs, l // blk),
        in_specs=[
            pl.BlockSpec((1, blk, LANES), lambda bi, p, qi: (bi, qi, p)),
            pl.BlockSpec((1, l, LANES), lambda bi, p, qi: (bi, 0, p)),
            pl.BlockSpec((1, l, LANES), lambda bi, p, qi: (bi, 0, p)),
            pl.BlockSpec((1, LANES), fixed),
            pl.BlockSpec((1, LANES), fixed),
            pl.BlockSpec((blk, blk), fixed),
        ],
        out_specs=pl.BlockSpec((1, blk, LANES), lambda bi, p, qi: (bi, qi, p)),
        out_shape=jax.ShapeDtypeStruct((b, l, d), BF16),
        scratch_shapes=[pltpu.VMEM((l, LANES), BF16)],
        compiler_params=pltpu.CompilerParams(
            dimension_semantics=("arbitrary", "arbitrary", "arbitrary"),
            vmem_limit_bytes=VMEM_LIMIT),
        name="sb_attn",
    )(q, k, v, qnw, knw, u_neg)


def _ssd_kernel(xbc_ref, dt_ref, z_ref, cw_ref, cb_ref, dtb_ref, alog_ref, dskip_ref, nw_ref,
                expand_ref, y_ref, ext_ref, state_ref, *, d_ssd):
    ci = pl.program_id(1)
    q = SSD_CHUNK
    gw = SSD_STATE
    heads_per_group = d_ssd // HEAD_DIM // SSD_GROUPS
    group_w = heads_per_group * HEAD_DIM

    @pl.when(ci == 0)
    def _():
        ext_ref[0:CONV_HALO, :] = jnp.zeros((CONV_HALO, ext_ref.shape[1]), F32)
        state_ref[...] = jnp.zeros(state_ref.shape, F32)

    @pl.when(ci > 0)
    def _():
        ext_ref[0:CONV_HALO, :] = ext_ref[q:q + CONV_HALO, :]

    ext_ref[CONV_HALO:CONV_HALO + q, :] = xbc_ref[0].astype(F32)

    conv = cb_ref[...]
    for i in range(CONV_W):
        s = CONV_HALO - (CONV_W - 1) + i
        conv = conv + ext_ref[s:s + q, :] * cw_ref[i:i + 1, :]
    u = _silu(conv)
    xs = u[:, :d_ssd]
    b_all = u[:, d_ssd:d_ssd + SSD_GROUPS * gw]
    c_all = u[:, d_ssd + SSD_GROUPS * gw:]

    dt = _softplus(dt_ref[0] + dtb_ref[...])
    d_a = dt * (-jnp.exp(alog_ref[...]))
    r_id = lax.broadcasted_iota(jnp.int32, (q, q), 0)
    c_id = lax.broadcasted_iota(jnp.int32, (q, q), 1)
    lower = c_id <= r_id
    tri = jnp.where(lower, 1.0, 0.0).astype(BF16)
    d_a_hi = d_a.astype(BF16)
    d_a_lo = (d_a - d_a_hi.astype(F32)).astype(BF16)
    a_cs = (jnp.dot(tri, d_a_hi, preferred_element_type=F32)
            + jnp.dot(tri, d_a_lo, preferred_element_type=F32))
    a_cs_t = a_cs.T
    a_last = a_cs[q - 1:q, :]

    def expand(t):
        return jnp.dot(t.astype(BF16), expand_ref[...], preferred_element_type=F32)

    dt_e = expand(dt)
    fac_e = expand(dt * jnp.exp(a_last - a_cs))
    ea_e = expand(jnp.exp(a_cs))

    xdt = xs * dt_e
    xfac = (xs * fac_e).astype(BF16)
    first = lax.broadcasted_iota(jnp.int32, (1, LANES), 1) < HEAD_DIM

    y_parts = []
    new_states = []
    for g in range(SSD_GROUPS):
        bg = b_all[:, g * gw:(g + 1) * gw]
        cg = c_all[:, g * gw:(g + 1) * gw].astype(BF16)
        cb = lax.dot_general(cg, bg.astype(BF16), (((1,), (1,)), ((), ())),
                             preferred_element_type=F32)
        bg_t = bg.T.astype(BF16)
        lo, hi = g * group_w, (g + 1) * group_w
        st_old = state_ref[:, lo:hi]
        y_off = jnp.dot(cg, st_old.astype(BF16), preferred_element_type=F32) * ea_e[:, lo:hi]
        new_states.append(st_old * ea_e[q - 1:q, lo:hi]
                          + jnp.dot(bg_t, xfac[:, lo:hi], preferred_element_type=F32))
        for m in range(heads_per_group // 2):
            base = lo + m * LANES
            x_pair = xdt[:, base:base + LANES]
            y_pair = None
            for h in range(2):
                kh = base // HEAD_DIM + h
                seg = a_cs[:, kh:kh + 1] - a_cs_t[kh:kh + 1, :]
                decay = jnp.where(lower, jnp.exp(jnp.minimum(seg, 0.0)), 0.0)
                wts = (cb * decay).astype(BF16)
                x_h = jnp.where(first if h == 0 else jnp.logical_not(first), x_pair, 0.0)
                part = jnp.dot(wts, x_h.astype(BF16), preferred_element_type=F32)
                y_pair = part if y_pair is None else y_pair + part
            y_parts.append(y_pair + y_off[:, m * LANES:(m + 1) * LANES])
    for g in range(SSD_GROUPS):
        state_ref[:, g * group_w:(g + 1) * group_w] = new_states[g]

    y = jnp.concatenate(y_parts, axis=1) + dskip_ref[...] * xs
    gated = y * _silu(z_ref[0].astype(F32))
    ms = jnp.mean(gated * gated, axis=-1, keepdims=True)
    y_ref[0] = (gated * lax.rsqrt(ms + EPS) * nw_ref[...]).astype(y_ref.dtype)


def _ssd(xbc, dt_raw, z_ssd, conv_w, conv_b, dt_bias, a_log, d_skip, ssd_norm_w, d_ssd):
    b, l, conv_dim = xbc.shape
    heads = d_ssd // HEAD_DIM
    q = SSD_CHUNK
    pad = LANES - heads
    dtb = jnp.pad(dt_bias.astype(F32), (0, pad)).reshape(1, LANES)
    alog = jnp.pad(a_log.astype(F32), (0, pad)).reshape(1, LANES)
    dskip_e = jnp.repeat(d_skip.astype(F32), HEAD_DIM).reshape(1, d_ssd)
    lane_head = lax.broadcasted_iota(jnp.int32, (LANES, d_ssd), 1) // HEAD_DIM
    row_head = lax.broadcasted_iota(jnp.int32, (LANES, d_ssd), 0)
    expand = jnp.where(lane_head == row_head, 1.0, 0.0).astype(BF16)
    chunk = lambda bi, ci: (bi, ci, 0)
    fixed = lambda bi, ci: (0, 0)
    return pl.pallas_call(
        functools.partial(_ssd_kernel, d_ssd=d_ssd),
        grid=(b, l // q),
        in_specs=[
            pl.BlockSpec((1, q, conv_dim), chunk),
            pl.BlockSpec((1, q, LANES), chunk),
            pl.BlockSpec((1, q, d_ssd), chunk),
            pl.BlockSpec((CONV_W, conv_dim), fixed),
            pl.BlockSpec((1, conv_dim), fixed),
            pl.BlockSpec((1, LANES), fixed),
            pl.BlockSpec((1, LANES), fixed),
            pl.BlockSpec((1, d_ssd), fixed),
            pl.BlockSpec((1, d_ssd), fixed),
            pl.BlockSpec((LANES, d_ssd), fixed),
        ],
        out_specs=pl.BlockSpec((1, q, d_ssd), chunk),
        out_shape=jax.ShapeDtypeStruct((b, l, d_ssd), BF16),
        scratch_shapes=[pltpu.VMEM((q + CONV_HALO, conv_dim), F32),
                        pltpu.VMEM((SSD_STATE, d_ssd), F32)],
        compiler_params=pltpu.CompilerParams(
            dimension_semantics=("arbitrary", "arbitrary"), vmem_limit_bytes=VMEM_LIMIT),
        name="ssd",
    )(xbc, dt_raw, z_ssd, conv_w.astype(F32), conv_b.astype(F32).reshape(1, conv_dim),
      dtb, alog, dskip_e, ssd_norm_w.astype(F32).reshape(1, d_ssd), expand)


def _out_proj_kernel(o_ref, z_ref, y_ref, x_ref, nw_ref, w_ref, out_ref):
    d_sb = o_ref.shape[-1]
    gated = o_ref[...].astype(F32) * _silu(z_ref[...].astype(F32))
    ms = jnp.mean(gated * gated, axis=-1, keepdims=True)
    y_sb = (gated * lax.rsqrt(ms + EPS) * nw_ref[...]).astype(BF16)
    out_ref[...] = (x_ref[...]
                    + jnp.dot(y_sb, w_ref[:d_sb, :], preferred_element_type=F32)
                    + jnp.dot(y_ref[...], w_ref[d_sb:, :], preferred_element_type=F32))


def _out_proj(o_sb, z_sb, y_ssd, x2, sb_norm_w, w_out):
    m, d = x2.shape
    d_sb = o_sb.shape[-1]
    d_ssd = y_ssd.shape[-1]
    row = lambda i: (i, 0)
    fixed = lambda i: (0, 0)
    return pl.pallas_call(
        _out_proj_kernel,
        grid=(m // ROW_TILE,),
        in_specs=[
            pl.BlockSpec((ROW_TILE, d_sb), row),
            pl.BlockSpec((ROW_TILE, d_sb), row),
            pl.BlockSpec((ROW_TILE, d_ssd), row),
            pl.BlockSpec((ROW_TILE, d), row),
            pl.BlockSpec((1, d_sb), fixed),
            pl.BlockSpec(w_out.shape, fixed, pipeline_mode=pl.Buffered(1)),
        ],
        out_specs=pl.BlockSpec((ROW_TILE, d), row),
        out_shape=jax.ShapeDtypeStruct((m, d), F32),
        compiler_params=pltpu.CompilerParams(
            dimension_semantics=("arbitrary",), vmem_limit_bytes=VMEM_LIMIT),
        name="out_proj",
    )(o_sb, z_sb, y_ssd, x2, sb_norm_w.astype(F32).reshape(1, d_sb), w_out)


def _layer(x, norm_w, w_in, q_norm_w, k_norm_w, conv_w, conv_b, dt_bias, a_log, d_skip,
           sb_norm_w, ssd_norm_w, w_out):
    b, l, d = x.shape
    d_sb = sb_norm_w.shape[0]
    d_ssd = ssd_norm_w.shape[0]
    conv_dim = conv_w.shape[1]
    heads = dt_bias.shape[0]
    main = 4 * d_sb + d_ssd + conv_dim
    assert w_in.shape[1] == main + heads and heads <= LANES
    assert conv_dim == d_ssd + 2 * SSD_GROUPS * SSD_STATE and heads * HEAD_DIM == d_ssd
    m = b * l
    x2 = x.reshape(m, d)
    w_main = w_in[:, :main].astype(BF16)
    w_dt = jnp.pad(w_in[:, main:], ((0, 0), (0, LANES - heads))).astype(BF16)
    q, k, v, z_sb, z_ssd, xbc, dt_raw = _in_proj(
        x2, norm_w.astype(F32), w_main, w_dt, d_sb, d_ssd, conv_dim)
    o_sb = _sb_attention(q.reshape(b, l, d_sb), k.reshape(b, l, d_sb), v.reshape(b, l, d_sb),
                         q_norm_w, k_norm_w)
    y_ssd = _ssd(xbc.reshape(b, l, conv_dim), dt_raw.reshape(b, l, LANES),
                 z_ssd.reshape(b, l, d_ssd), conv_w, conv_b, dt_bias, a_log, d_skip,
                 ssd_norm_w, d_ssd)
    out = _out_proj(o_sb.reshape(m, d_sb), z_sb, y_ssd.reshape(m, d_ssd), x2, sb_norm_w,
                    w_out.astype(BF16))
    return out.reshape(b, l, d)


def kernel(x, norm_w, w_in, q_norm_w, k_norm_w, conv_w, conv_b, dt_bias, A_log, D_skip,
           sb_norm_w, ssd_norm_w, w_out):
    for layer in range(norm_w.shape[0]):
        x = _layer(x, norm_w[layer], w_in[layer], q_norm_w[layer], k_norm_w[layer],
                   conv_w[layer], conv_b[layer], dt_bias[layer], A_log[layer], D_skip[layer],
                   sb_norm_w[layer], ssd_norm_w[layer], w_out[layer])
    return x
```

```python
import functools
import math

import jax
import jax.numpy as jnp
from jax import lax
from jax.experimental import pallas as pl
from jax.experimental.pallas import tpu as pltpu

F32 = jnp.float32
BF16 = jnp.bfloat16

EPS = 1e-6
LANES = 128
HEAD_DIM = 64
SSD_STATE = 128
SSD_GROUPS = 2
SSD_CHUNK = 128
CONV_W = 4
CONV_HALO = 8

ROW_TILE = 512
ATT_BLOCK = 256
VMEM_LIMIT = 56 * 1024 * 1024
LOG_STICK_FLOOR = -104.0
MASKED_LOG = -1e30


def _silu(z):
    return z * (1.0 / (1.0 + jnp.exp(-z)))


def _softplus(z):
    return jnp.maximum(z, 0.0) + jnp.log(1.0 + jnp.exp(-jnp.abs(z)))


def _in_proj_kernel(x_ref, nw_ref, w_ref, wdt_ref,
                    q_ref, k_ref, v_ref, zsb_ref, zssd_ref, xbc_ref, dt_ref):
    x = x_ref[...]
    ms = jnp.mean(x * x, axis=-1, keepdims=True)
    hn = (x * lax.rsqrt(ms + EPS) * nw_ref[...]).astype(BF16)
    start = 0
    for ref in (q_ref, k_ref, v_ref, zsb_ref, zssd_ref, xbc_ref):
        width = ref.shape[-1]
        ref[...] = jnp.dot(hn, w_ref[:, start:start + width],
                           preferred_element_type=F32).astype(ref.dtype)
        start += width
    dt_ref[...] = jnp.dot(hn, wdt_ref[...], preferred_element_type=F32)


def _in_proj(x2, norm_w, w_main, w_dt, d_sb, d_ssd, conv_dim):
    m, d = x2.shape
    widths = (d_sb, d_sb, d_sb, d_sb, d_ssd, conv_dim)
    row = lambda i: (i, 0)
    fixed = lambda i: (0, 0)
    out_shape = [jax.ShapeDtypeStruct((m, w), BF16) for w in widths]
    out_shape.append(jax.ShapeDtypeStruct((m, LANES), F32))
    out_specs = [pl.BlockSpec((ROW_TILE, w), row) for w in widths]
    out_specs.append(pl.BlockSpec((ROW_TILE, LANES), row))
    return pl.pallas_call(
        _in_proj_kernel,
        grid=(m // ROW_TILE,),
        in_specs=[
            pl.BlockSpec((ROW_TILE, d), row),
            pl.BlockSpec((1, d), fixed),
            pl.BlockSpec(w_main.shape, fixed, pipeline_mode=pl.Buffered(1)),
            pl.BlockSpec(w_dt.shape, fixed, pipeline_mode=pl.Buffered(1)),
        ],
        out_specs=out_specs,
        out_shape=out_shape,
        compiler_params=pltpu.CompilerParams(
            dimension_semantics=("arbitrary",), vmem_limit_bytes=VMEM_LIMIT),
        name="in_proj",
    )(x2, norm_w.reshape(1, d), w_main, w_dt)


def _attn_kernel(q_ref, k_ref, v_ref, qnw_ref, knw_ref, u_ref, o_ref, kn_ref, *, seq_len):
    blk = ATT_BLOCK
    first = lax.broadcasted_iota(jnp.int32, (1, LANES), 1) < HEAD_DIM
    r_id = lax.broadcasted_iota(jnp.int32, (blk, blk), 0)
    c_id = lax.broadcasted_iota(jnp.int32, (blk, blk), 1)
    causal = c_id < r_id

    def head_norm(t, w):
        sq = t * t
        s0 = jnp.sum(jnp.where(first, sq, 0.0), axis=1, keepdims=True)
        s1 = jnp.sum(jnp.where(first, 0.0, sq), axis=1, keepdims=True)
        ms = jnp.where(first, s0, s1) * (1.0 / HEAD_DIM)
        return t * lax.rsqrt(ms + EPS) * w

    def norm_keys(i, carry):
        rows = 2 * blk
        r = pl.multiple_of(i * rows, rows)
        kk = k_ref[0, pl.ds(r, rows), :].astype(F32)
        kn_ref[pl.ds(r, rows), :] = head_norm(kk, knw_ref[...]).astype(BF16)
        return carry

    lax.fori_loop(0, seq_len // (2 * blk), norm_keys, 0)

    def key_block(qm, j, masked):
        off = pl.multiple_of(j * blk, blk)
        kb = kn_ref[pl.ds(off, blk), :]
        vb = v_ref[0, pl.ds(off, blk), :]
        z = lax.dot_general(qm, kb, (((1,), (1,)), ((), ())), preferred_element_type=F32)
        sp = _softplus(z)
        lq = z - sp
        if masked:
            sp = jnp.where(causal, sp, 0.0)
        rl = jnp.dot(sp.astype(BF16), u_ref[...], preferred_element_type=F32)
        w = jnp.exp(lq + rl)
        if masked:
            w = jnp.where(causal, w, 0.0)
        pv = jnp.dot(w.astype(BF16), vb, preferred_element_type=F32)
        return pv, jnp.sum(sp, axis=1, keepdims=True)

    def query_block(qi, carry):
        q_off = pl.multiple_of(qi * blk, blk)
        qn = head_norm(q_ref[0, pl.ds(q_off, blk), :].astype(F32), qnw_ref[...])
        qn = qn * (1.0 / math.sqrt(HEAD_DIM))
        qms = [jnp.where(first if h == 0 else jnp.logical_not(first), qn, 0.0).astype(BF16)
               for h in range(2)]

        has_prev = qi > 0
        j_prev = jnp.maximum(qi - 1, 0)
        accs, cs = [], []
        for h in range(2):
            pv_d, s_d = key_block(qms[h], qi, True)
            pv_p, s_p = key_block(qms[h], j_prev, False)
            c = -s_d
            accs.append(pv_d + jnp.where(has_prev, jnp.exp(c) * pv_p, 0.0))
            cs.append(c - jnp.where(has_prev, s_p, 0.0))

        def log_stick_left(cs):
            return jnp.max(jnp.maximum(cs[0], cs[1]))

        def cond(state):
            return jnp.logical_and(state[0] >= 0, state[1] > LOG_STICK_FLOOR)

        def body(state):
            j, _, a0, a1, c0, c1 = state
            accs, cs = [], []
            for h, (a, c) in enumerate(((a0, c0), (a1, c1))):
                pv, s = key_block(qms[h], j, False)
                accs.append(a + jnp.exp(c) * pv)
                cs.append(c - s)
            return (j - 1, log_stick_left(cs), accs[0], accs[1], cs[0], cs[1])

        state = lax.while_loop(
            cond, body, (qi - 2, log_stick_left(cs), accs[0], accs[1], cs[0], cs[1]))
        o_ref[0, pl.ds(q_off, blk), :] = jnp.where(first, state[2], state[3]).astype(o_ref.dtype)
        return carry

    lax.fori_loop(0, seq_len // blk, query_block, 0)


def _sb_attention(q, k, v, q_norm_w, k_norm_w):
    b, l, d = q.shape
    blk = ATT_BLOCK
    r_id = lax.broadcasted_iota(jnp.int32, (blk, blk), 0)
    c_id = lax.broadcasted_iota(jnp.int32, (blk, blk), 1)
    u_neg = jnp.where(r_id > c_id, -1.0, 0.0).astype(BF16)
    qnw = jnp.tile(q_norm_w.astype(F32), LANES // HEAD_DIM).reshape(1, LANES)
    knw = jnp.tile(k_norm_w.astype(F32), LANES // HEAD_DIM).reshape(1, LANES)
    pair = lambda bi, p: (bi, 0, p)
    fixed = lambda bi, p: (0, 0)
    return pl.pallas_call(
        functools.partial(_attn_kernel, seq_len=l),
        grid=(b, d // LANES),
        in_specs=[
            pl.BlockSpec((1, l, LANES), pair),
            pl.BlockSpec((1, l, LANES), pair),
            pl.BlockSpec((1, l, LANES), pair),
            pl.BlockSpec((1, LANES), fixed),
            pl.BlockSpec((1, LANES), fixed),
            pl.BlockSpec((blk, blk), fixed),
        ],
        out_specs=pl.BlockSpec((1, l, LANES), pair),
        out_shape=jax.ShapeDtypeStruct((b, l, d), BF16),
        scratch_shapes=[pltpu.VMEM((l, LANES), BF16)],
        compiler_params=pltpu.CompilerParams(
            dimension_semantics=("arbitrary", "arbitrary"), vmem_limit_bytes=VMEM_LIMIT),
        name="sb_attn",
    )(q, k, v, qnw, knw, u_neg)


def _ssd_kernel(xbc_ref, dt_ref, z_ref, cw_ref, cb_ref, dtb_ref, alog_ref, dskip_ref, nw_ref,
                expand_ref, y_ref, ext_ref, state_ref, *, d_ssd):
    ci = pl.program_id(1)
    q = SSD_CHUNK
    gw = SSD_STATE
    heads_per_group = d_ssd // HEAD_DIM // SSD_GROUPS
    group_w = heads_per_group * HEAD_DIM

    @pl.when(ci == 0)
    def _():
        ext_ref[0:CONV_HALO, :] = jnp.zeros((CONV_HALO, ext_ref.shape[1]), F32)
        state_ref[...] = jnp.zeros(state_ref.shape, F32)

    @pl.when(ci > 0)
    def _():
        ext_ref[0:CONV_HALO, :] = ext_ref[q:q + CONV_HALO, :]

    ext_ref[CONV_HALO:CONV_HALO + q, :] = xbc_ref[0].astype(F32)

    ext = ext_ref[...]
    conv = cb_ref[...] + ext[CONV_HALO:, :] * cw_ref[CONV_W - 1:CONV_W, :]
    for back in range(1, CONV_W):
        tap = CONV_W - 1 - back
        conv = conv + pltpu.roll(ext, back, axis=0)[CONV_HALO:, :] * cw_ref[tap:tap + 1, :]
    u = _silu(conv)
    xs = u[:, :d_ssd]
    b_all = u[:, d_ssd:d_ssd + SSD_GROUPS * gw]
    c_all = u[:, d_ssd + SSD_GROUPS * gw:]

    dt = _softplus(dt_ref[0] + dtb_ref[...])
    d_a = dt * (-jnp.exp(alog_ref[...]))
    r_id = lax.broadcasted_iota(jnp.int32, (q, q), 0)
    c_id = lax.broadcasted_iota(jnp.int32, (q, q), 1)
    lower = c_id <= r_id
    tri = jnp.where(lower, 1.0, 0.0).astype(BF16)
    d_a_hi = d_a.astype(BF16)
    d_a_lo = (d_a - d_a_hi.astype(F32)).astype(BF16)
    a_cs = (jnp.dot(tri, d_a_hi, preferred_element_type=F32)
            + jnp.dot(tri, d_a_lo, preferred_element_type=F32))
    a_cs_t = a_cs.T
    a_last = a_cs[q - 1:q, :]

    def expand(t):
        return jnp.dot(t.astype(BF16), expand_ref[...], preferred_element_type=F32)

    dt_e = expand(dt)
    fac_e = expand(dt * jnp.exp(a_last - a_cs))
    ea_e = expand(jnp.exp(a_cs))

    xdt = xs * dt_e
    xfac = (xs * fac_e).astype(BF16)
    first = lax.broadcasted_iota(jnp.int32, (1, LANES), 1) < HEAD_DIM

    y_parts = []
    new_states = []
    for g in range(SSD_GROUPS):
        bg = b_all[:, g * gw:(g + 1) * gw]
        cg = c_all[:, g * gw:(g + 1) * gw].astype(BF16)
        cb = lax.dot_general(cg, bg.astype(BF16), (((1,), (1,)), ((), ())),
                             preferred_element_type=F32)
        bg_t = bg.T.astype(BF16)
        lo, hi = g * group_w, (g + 1) * group_w
        st_old = state_ref[:, lo:hi]
        y_off = jnp.dot(cg, st_old.astype(BF16), preferred_element_type=F32) * ea_e[:, lo:hi]
        new_states.append(st_old * ea_e[q - 1:q, lo:hi]
                          + jnp.dot(bg_t, xfac[:, lo:hi], preferred_element_type=F32))
        for m in range(heads_per_group // 2):
            base = lo + m * LANES
            x_pair = xdt[:, base:base + LANES]
            y_pair = None
            for h in range(2):
                kh = base // HEAD_DIM + h
                seg = a_cs[:, kh:kh + 1] - a_cs_t[kh:kh + 1, :]
                decay = jnp.exp(jnp.where(lower, seg, MASKED_LOG))
                wts = (cb * decay).astype(BF16)
                x_h = jnp.where(first if h == 0 else jnp.logical_not(first), x_pair, 0.0)
                part = jnp.dot(wts, x_h.astype(BF16), preferred_element_type=F32)
                y_pair = part if y_pair is None else y_pair + part
            y_parts.append(y_pair + y_off[:, m * LANES:(m + 1) * LANES])
    for g in range(SSD_GROUPS):
        state_ref[:, g * group_w:(g + 1) * group_w] = new_states[g]

    y = jnp.concatenate(y_parts, axis=1) + dskip_ref[...] * xs
    gated = y * _silu(z_ref[0].astype(F32))
    ms = jnp.mean(gated * gated, axis=-1, keepdims=True)
    y_ref[0] = (gated * lax.rsqrt(ms + EPS) * nw_ref[...]).astype(y_ref.dtype)


def _ssd(xbc, dt_raw, z_ssd, conv_w, conv_b, dt_bias, a_log, d_skip, ssd_norm_w, d_ssd):
    b, l, conv_dim = xbc.shape
    heads = d_ssd // HEAD_DIM
    q = SSD_CHUNK
    pad = LANES - heads
    dtb = jnp.pad(dt_bias.astype(F32), (0, pad)).reshape(1, LANES)
    alog = jnp.pad(a_log.astype(F32), (0, pad)).reshape(1, LANES)
    dskip_e = jnp.repeat(d_skip.astype(F32), HEAD_DIM).reshape(1, d_ssd)
    lane_head = lax.broadcasted_iota(jnp.int32, (LANES, d_ssd), 1) // HEAD_DIM
    row_head = lax.broadcasted_iota(jnp.int32, (LANES, d_ssd), 0)
    expand = jnp.where(lane_head == row_head, 1.0, 0.0).astype(BF16)
    chunk = lambda bi, ci: (bi, ci, 0)
    fixed = lambda bi, ci: (0, 0)
    return pl.pallas_call(
        functools.partial(_ssd_kernel, d_ssd=d_ssd),
        grid=(b, l // q),
        in_specs=[
            pl.BlockSpec((1, q, conv_dim), chunk),
            pl.BlockSpec((1, q, LANES), chunk),
            pl.BlockSpec((1, q, d_ssd), chunk),
            pl.BlockSpec((CONV_W, conv_dim), fixed),
            pl.BlockSpec((1, conv_dim), fixed),
            pl.BlockSpec((1, LANES), fixed),
            pl.BlockSpec((1, LANES), fixed),
            pl.BlockSpec((1, d_ssd), fixed),
            pl.BlockSpec((1, d_ssd), fixed),
            pl.BlockSpec((LANES, d_ssd), fixed),
        ],
        out_specs=pl.BlockSpec((1, q, d_ssd), chunk),
        out_shape=jax.ShapeDtypeStruct((b, l, d_ssd), BF16),
        scratch_shapes=[pltpu.VMEM((q + CONV_HALO, conv_dim), F32),
                        pltpu.VMEM((SSD_STATE, d_ssd), F32)],
        compiler_params=pltpu.CompilerParams(
            dimension_semantics=("arbitrary", "arbitrary"), vmem_limit_bytes=VMEM_LIMIT),
        name="ssd",
    )(xbc, dt_raw, z_ssd, conv_w.astype(F32), conv_b.astype(F32).reshape(1, conv_dim),
      dtb, alog, dskip_e, ssd_norm_w.astype(F32).reshape(1, d_ssd), expand)


def _out_proj_kernel(o_ref, z_ref, y_ref, x_ref, nw_ref, w_ref, out_ref):
    d_sb = o_ref.shape[-1]
    gated = o_ref[...].astype(F32) * _silu(z_ref[...].astype(F32))
    ms = jnp.mean(gated * gated, axis=-1, keepdims=True)
    y_sb = (gated * lax.rsqrt(ms + EPS) * nw_ref[...]).astype(BF16)
    out_ref[...] = (x_ref[...]
                    + jnp.dot(y_sb, w_ref[:d_sb, :], preferred_element_type=F32)
                    + jnp.dot(y_ref[...], w_ref[d_sb:, :], preferred_element_type=F32))


def _out_proj(o_sb, z_sb, y_ssd, x2, sb_norm_w, w_out):
    m, d = x2.shape
    d_sb = o_sb.shape[-1]
    d_ssd = y_ssd.shape[-1]
    row = lambda i: (i, 0)
    fixed = lambda i: (0, 0)
    return pl.pallas_call(
        _out_proj_kernel,
        grid=(m // ROW_TILE,),
        in_specs=[
            pl.BlockSpec((ROW_TILE, d_sb), row),
            pl.BlockSpec((ROW_TILE, d_sb), row),
            pl.BlockSpec((ROW_TILE, d_ssd), row),
            pl.BlockSpec((ROW_TILE, d), row),
            pl.BlockSpec((1, d_sb), fixed),
            pl.BlockSpec(w_out.shape, fixed, pipeline_mode=pl.Buffered(1)),
        ],
        out_specs=pl.BlockSpec((ROW_TILE, d), row),
        out_shape=jax.ShapeDtypeStruct((m, d), F32),
        compiler_params=pltpu.CompilerParams(
            dimension_semantics=("arbitrary",), vmem_limit_bytes=VMEM_LIMIT),
        name="out_proj",
    )(o_sb, z_sb, y_ssd, x2, sb_norm_w.astype(F32).reshape(1, d_sb), w_out)


def _layer(x, norm_w, w_in, q_norm_w, k_norm_w, conv_w, conv_b, dt_bias, a_log, d_skip,
           sb_norm_w, ssd_norm_w, w_out):
    b, l, d = x.shape
    d_sb = sb_norm_w.shape[0]
    d_ssd = ssd_norm_w.shape[0]
    conv_dim = conv_w.shape[1]
    heads = dt_bias.shape[0]
    main = 4 * d_sb + d_ssd + conv_dim
    assert w_in.shape[1] == main + heads and heads <= LANES
    assert conv_dim == d_ssd + 2 * SSD_GROUPS * SSD_STATE and heads * HEAD_DIM == d_ssd
    m = b * l
    x2 = x.reshape(m, d)
    w_main = w_in[:, :main].astype(BF16)
    w_dt = jnp.pad(w_in[:, main:], ((0, 0), (0, LANES - heads))).astype(BF16)
    q, k, v, z_sb, z_ssd, xbc, dt_raw = _in_proj(
        x2, norm_w.astype(F32), w_main, w_dt, d_sb, d_ssd, conv_dim)
    o_sb = _sb_attention(q.reshape(b, l, d_sb), k.reshape(b, l, d_sb), v.reshape(b, l, d_sb),
                         q_norm_w, k_norm_w)
    y_ssd = _ssd(xbc.reshape(b, l, conv_dim), dt_raw.reshape(b, l, LANES),
                 z_ssd.reshape(b, l, d_ssd), conv_w, conv_b, dt_bias, a_log, d_skip,
                 ssd_norm_w, d_ssd)
    out = _out_proj(o_sb.reshape(m, d_sb), z_sb, y_ssd.reshape(m, d_ssd), x2, sb_norm_w,
                    w_out.astype(BF16))
    return out.reshape(b, l, d)


def kernel(x, norm_w, w_in, q_norm_w, k_norm_w, conv_w, conv_b, dt_bias, A_log, D_skip,
           sb_norm_w, ssd_norm_w, w_out):
    for layer in range(norm_w.shape[0]):
        x = _layer(x, norm_w[layer], w_in[layer], q_norm_w[layer], k_norm_w[layer],
                   conv_w[layer], conv_b[layer], dt_bias[layer], A_log[layer], D_skip[layer],
                   sb_norm_w[layer], ssd_norm_w[layer], w_out[layer])
    return x
```

```python
import functools
import math

import jax
import jax.numpy as jnp
from jax import lax
from jax.experimental import pallas as pl
from jax.experimental.pallas import tpu as pltpu

F32 = jnp.float32
BF16 = jnp.bfloat16

EPS = 1e-6
LANES = 128
HEAD_DIM = 64
SSD_STATE = 128
SSD_GROUPS = 2
SSD_CHUNK = 128
CONV_W = 4
CONV_HALO = 8

ROW_TILE = 512
ATT_BLOCK = 256
ATT_UNROLL = 4
VMEM_LIMIT = 56 * 1024 * 1024
LOG2E = 1.4426950408889634
LOG2_STICK_FLOOR = -150.1
MASKED_LOG = -1e30


def _silu(z):
    return z * (1.0 / (1.0 + jnp.exp(-z)))


def _softplus(z):
    return jnp.maximum(z, 0.0) + jnp.log(1.0 + jnp.exp(-jnp.abs(z)))


def _in_proj_kernel(x_ref, nw_ref, w_ref, wdt_ref,
                    q_ref, k_ref, v_ref, zsb_ref, zssd_ref, xbc_ref, dt_ref):
    x = x_ref[...]
    ms = jnp.mean(x * x, axis=-1, keepdims=True)
    hn = (x * lax.rsqrt(ms + EPS) * nw_ref[...]).astype(BF16)
    start = 0
    for ref in (q_ref, k_ref, v_ref, zsb_ref, zssd_ref, xbc_ref):
        width = ref.shape[-1]
        ref[...] = jnp.dot(hn, w_ref[:, start:start + width],
                           preferred_element_type=F32).astype(ref.dtype)
        start += width
    dt_ref[...] = jnp.dot(hn, wdt_ref[...], preferred_element_type=F32)


def _in_proj(x2, norm_w, w_main, w_dt, d_sb, d_ssd, conv_dim):
    m, d = x2.shape
    widths = (d_sb, d_sb, d_sb, d_sb, d_ssd, conv_dim)
    row = lambda i: (i, 0)
    fixed = lambda i: (0, 0)
    out_shape = [jax.ShapeDtypeStruct((m, w), BF16) for w in widths]
    out_shape.append(jax.ShapeDtypeStruct((m, LANES), F32))
    out_specs = [pl.BlockSpec((ROW_TILE, w), row) for w in widths]
    out_specs.append(pl.BlockSpec((ROW_TILE, LANES), row))
    return pl.pallas_call(
        _in_proj_kernel,
        grid=(m // ROW_TILE,),
        in_specs=[
            pl.BlockSpec((ROW_TILE, d), row),
            pl.BlockSpec((1, d), fixed),
            pl.BlockSpec(w_main.shape, fixed, pipeline_mode=pl.Buffered(1)),
            pl.BlockSpec(w_dt.shape, fixed, pipeline_mode=pl.Buffered(1)),
        ],
        out_specs=out_specs,
        out_shape=out_shape,
        compiler_params=pltpu.CompilerParams(
            dimension_semantics=("arbitrary",), vmem_limit_bytes=VMEM_LIMIT),
        name="in_proj",
    )(x2, norm_w.reshape(1, d), w_main, w_dt)


def _attn_kernel(q_ref, k_ref, v_ref, qnw_ref, knw_ref, u_ref, o_ref, kn_ref, *, seq_len):
    blk = ATT_BLOCK
    first = lax.broadcasted_iota(jnp.int32, (1, LANES), 1) < HEAD_DIM
    r_id = lax.broadcasted_iota(jnp.int32, (blk, blk), 0)
    c_id = lax.broadcasted_iota(jnp.int32, (blk, blk), 1)
    causal = c_id < r_id

    same_head = (lax.broadcasted_iota(jnp.int32, (LANES, LANES), 0) < HEAD_DIM) == first
    head_ones = jnp.where(same_head, 1.0, 0.0).astype(BF16)

    def head_norm(t, w):
        ss = jnp.dot((t * t).astype(BF16), head_ones, preferred_element_type=F32)
        return t * lax.rsqrt(ss * (1.0 / HEAD_DIM) + EPS) * w

    def norm_keys(i, carry):
        rows = 4 * blk
        r = pl.multiple_of(i * rows, rows)
        kk = k_ref[0, pl.ds(r, rows), :].astype(F32)
        kn_ref[pl.ds(r, rows), :] = head_norm(kk, knw_ref[...]).astype(BF16)
        return carry

    lax.fori_loop(0, seq_len // (4 * blk), norm_keys, 0)

    def stage_scores(qm, j):
        kb = kn_ref[pl.ds(pl.multiple_of(j * blk, blk), blk), :]
        return lax.dot_general(qm, kb, (((1,), (1,)), ((), ())), preferred_element_type=F32)

    def log_terms(z):
        sp = jnp.maximum(z, 0.0) + jnp.log(1.0 + jnp.exp2(-jnp.abs(z))) * LOG2E
        return sp, z - sp

    def stage_log_terms(z, diag):
        sp, lq = log_terms(z)
        if diag:
            sp = jnp.where(causal, sp, 0.0)
        return sp.astype(BF16), lq, jnp.sum(sp, axis=1, keepdims=True)

    def stage_cumsum(sp_bf):
        return jnp.dot(sp_bf, u_ref[...], preferred_element_type=F32)

    def stage_weights(lq, rl, diag):
        w = jnp.exp2(lq + rl)
        if diag:
            w = jnp.where(causal, w, 0.0)
        return w.astype(BF16)

    def stage_values(w_bf, j):
        vb = v_ref[0, pl.ds(pl.multiple_of(j * blk, blk), blk), :]
        return jnp.dot(w_bf, vb, preferred_element_type=F32)

    def key_block(qm, j, masked):
        sp_bf, lq, s = stage_log_terms(stage_scores(qm, j), masked)
        w_bf = stage_weights(lq, stage_cumsum(sp_bf), masked)
        return stage_values(w_bf, j), s

    def log_stick_left(cs):
        return jnp.max(jnp.maximum(cs[0], cs[1]))

    def query_blocks(it, carry):
        pairs = []
        for b in range(ATT_UNROLL):
            qi = it * ATT_UNROLL + b
            q_off = pl.multiple_of(qi * blk, blk)
            qn = head_norm(q_ref[0, pl.ds(q_off, blk), :].astype(F32), qnw_ref[...])
            qn = qn * (LOG2E / math.sqrt(HEAD_DIM))
            for h in range(2):
                qm = jnp.where(first if h == 0 else jnp.logical_not(first), qn, 0.0).astype(BF16)
                pairs.append({"qi": qi, "qm": qm, "js": (qi, jnp.maximum(qi - 1, 0))})
        diag = (True, False)
        for step in range(len(pairs) + 4):
            for lag, p in enumerate(pairs):
                stage = step - lag
                if stage == 0:
                    p["z"] = [stage_scores(p["qm"], j) for j in p["js"]]
                elif stage == 1:
                    p["log"] = [stage_log_terms(z, m) for z, m in zip(p["z"], diag)]
                elif stage == 2:
                    p["rl"] = [stage_cumsum(t[0]) for t in p["log"]]
                elif stage == 3:
                    p["w"] = [stage_weights(t[1], rl, m)
                              for t, rl, m in zip(p["log"], p["rl"], diag)]
                elif stage == 4:
                    p["pv"] = [stage_values(w, j) for w, j in zip(p["w"], p["js"])]

        starts = []
        for b in range(ATT_UNROLL):
            qi = pairs[2 * b]["qi"]
            has_prev = qi > 0
            accs, cs = [], []
            for p in pairs[2 * b:2 * b + 2]:
                c = -p["log"][0][2]
                accs.append(p["pv"][0] + jnp.where(has_prev, jnp.exp2(c) * p["pv"][1], 0.0))
                cs.append(c - jnp.where(has_prev, p["log"][1][2], 0.0))
            starts.append((qi - 2, log_stick_left(cs), accs[0], accs[1], cs[0], cs[1]))

        def cond(state):
            return jnp.logical_and(state[0] >= 0, state[1] > LOG2_STICK_FLOOR)

        for b in range(ATT_UNROLL):
            qms = [pairs[2 * b]["qm"], pairs[2 * b + 1]["qm"]]

            def body(state, qms=qms):
                j, _, a0, a1, c0, c1 = state
                accs, cs = [], []
                for h, (a, c) in enumerate(((a0, c0), (a1, c1))):
                    pv, s = key_block(qms[h], j, False)
                    accs.append(a + jnp.exp2(c) * pv)
                    cs.append(c - s)
                return (j - 1, log_stick_left(cs), accs[0], accs[1], cs[0], cs[1])

            state = lax.while_loop(cond, body, starts[b])
            q_off = pl.multiple_of(pairs[2 * b]["qi"] * blk, blk)
            o_ref[0, pl.ds(q_off, blk), :] = (
                jnp.where(first, state[2], state[3]).astype(o_ref.dtype))
        return carry

    lax.fori_loop(0, seq_len // (blk * ATT_UNROLL), query_blocks, 0)


def _sb_attention(q, k, v, q_norm_w, k_norm_w):
    b, l, d = q.shape
    blk = ATT_BLOCK
    r_id = lax.broadcasted_iota(jnp.int32, (blk, blk), 0)
    c_id = lax.broadcasted_iota(jnp.int32, (blk, blk), 1)
    u_neg = jnp.where(r_id > c_id, -1.0, 0.0).astype(BF16)
    qnw = jnp.tile(q_norm_w.astype(F32), LANES // HEAD_DIM).reshape(1, LANES)
    knw = jnp.tile(k_norm_w.astype(F32), LANES // HEAD_DIM).reshape(1, LANES)
    pair = lambda bi, p: (bi, 0, p)
    fixed = lambda bi, p: (0, 0)
    return pl.pallas_call(
        functools.partial(_attn_kernel, seq_len=l),
        grid=(b, d // LANES),
        in_specs=[
            pl.BlockSpec((1, l, LANES), pair),
            pl.BlockSpec((1, l, LANES), pair),
            pl.BlockSpec((1, l, LANES), pair),
            pl.BlockSpec((1, LANES), fixed),
            pl.BlockSpec((1, LANES), fixed),
            pl.BlockSpec((blk, blk), fixed),
        ],
        out_specs=pl.BlockSpec((1, l, LANES), pair),
        out_shape=jax.ShapeDtypeStruct((b, l, d), BF16),
        scratch_shapes=[pltpu.VMEM((l, LANES), BF16)],
        compiler_params=pltpu.CompilerParams(
            dimension_semantics=("arbitrary", "arbitrary"), vmem_limit_bytes=VMEM_LIMIT),
        name="sb_attn",
    )(q, k, v, qnw, knw, u_neg)


def _ssd_kernel(xbc_ref, dt_ref, z_ref, cw_ref, cb_ref, dtb_ref, alog_ref, dskip_ref, nw_ref,
                expand_ref, y_ref, ext_ref, state_ref, *, d_ssd):
    ci = pl.program_id(1)
    q = SSD_CHUNK
    gw = SSD_STATE
    heads_per_group = d_ssd // HEAD_DIM // SSD_GROUPS
    group_w = heads_per_group * HEAD_DIM

    @pl.when(ci == 0)
    def _():
        ext_ref[0:CONV_HALO, :] = jnp.zeros((CONV_HALO, ext_ref.shape[1]), F32)
        state_ref[...] = jnp.zeros(state_ref.shape, F32)

    @pl.when(ci > 0)
    def _():
        ext_ref[0:CONV_HALO, :] = ext_ref[q:q + CONV_HALO, :]

    ext_ref[CONV_HALO:CONV_HALO + q, :] = xbc_ref[0].astype(F32)

    ext = ext_ref[...]
    conv = cb_ref[...] + ext[CONV_HALO:, :] * cw_ref[CONV_W - 1:CONV_W, :]
    for back in range(1, CONV_W):
        tap = CONV_W - 1 - back
        conv = conv + pltpu.roll(ext, back, axis=0)[CONV_HALO:, :] * cw_ref[tap:tap + 1, :]
    u = _silu(conv)
    xs = u[:, :d_ssd]
    b_all = u[:, d_ssd:d_ssd + SSD_GROUPS * gw]
    c_all = u[:, d_ssd + SSD_GROUPS * gw:]

    dt = _softplus(dt_ref[0] + dtb_ref[...])
    d_a = dt * (-jnp.exp(alog_ref[...]))
    r_id = lax.broadcasted_iota(jnp.int32, (q, q), 0)
    c_id = lax.broadcasted_iota(jnp.int32, (q, q), 1)
    lower = c_id <= r_id
    tri = jnp.where(lower, 1.0, 0.0).astype(BF16)
    d_a_hi = d_a.astype(BF16)
    d_a_lo = (d_a - d_a_hi.astype(F32)).astype(BF16)
    a_cs = (jnp.dot(tri, d_a_hi, preferred_element_type=F32)
            + jnp.dot(tri, d_a_lo, preferred_element_type=F32))
    a_cs_t = a_cs.T
    a_last = a_cs[q - 1:q, :]

    def expand(t):
        return jnp.dot(t.astype(BF16), expand_ref[...], preferred_element_type=F32)

    dt_e = expand(dt)
    fac_e = expand(dt * jnp.exp(a_last - a_cs))
    ea_e = expand(jnp.exp(a_cs))

    xdt = xs * dt_e
    xfac = (xs * fac_e).astype(BF16)
    first = lax.broadcasted_iota(jnp.int32, (1, LANES), 1) < HEAD_DIM

    y_parts = []
    new_states = []
    for g in range(SSD_GROUPS):
        bg = b_all[:, g * gw:(g + 1) * gw]
        cg = c_all[:, g * gw:(g + 1) * gw].astype(BF16)
        cb = lax.dot_general(cg, bg.astype(BF16), (((1,), (1,)), ((), ())),
                             preferred_element_type=F32)
        bg_t = bg.T.astype(BF16)
        lo, hi = g * group_w, (g + 1) * group_w
        st_old = state_ref[:, lo:hi]
        y_off = jnp.dot(cg, st_old.astype(BF16), preferred_element_type=F32) * ea_e[:, lo:hi]
        new_states.append(st_old * ea_e[q - 1:q, lo:hi]
                          + jnp.dot(bg_t, xfac[:, lo:hi], preferred_element_type=F32))
        for m in range(heads_per_group // 2):
            base = lo + m * LANES
            x_pair = xdt[:, base:base + LANES]
            y_pair = None
            for h in range(2):
                kh = base // HEAD_DIM + h
                seg = a_cs[:, kh:kh + 1] - a_cs_t[kh:kh + 1, :]
                decay = jnp.exp(jnp.where(lower, seg, MASKED_LOG))
                wts = (cb * decay).astype(BF16)
                x_h = jnp.where(first if h == 0 else jnp.logical_not(first), x_pair, 0.0)
                part = jnp.dot(wts, x_h.astype(BF16), preferred_element_type=F32)
                y_pair = part if y_pair is None else y_pair + part
            y_parts.append(y_pair + y_off[:, m * LANES:(m + 1) * LANES])
    for g in range(SSD_GROUPS):
        state_ref[:, g * group_w:(g + 1) * group_w] = new_states[g]

    y = jnp.concatenate(y_parts, axis=1) + dskip_ref[...] * xs
    gated = y * _silu(z_ref[0].astype(F32))
    ms = jnp.mean(gated * gated, axis=-1, keepdims=True)
    y_ref[0] = (gated * lax.rsqrt(ms + EPS) * nw_ref[...]).astype(y_ref.dtype)


def _ssd(xbc, dt_raw, z_ssd, conv_w, conv_b, dt_bias, a_log, d_skip, ssd_norm_w, d_ssd):
    b, l, conv_dim = xbc.shape
    heads = d_ssd // HEAD_DIM
    q = SSD_CHUNK
    pad = LANES - heads
    dtb = jnp.pad(dt_bias.astype(F32), (0, pad)).reshape(1, LANES)
    alog = jnp.pad(a_log.astype(F32), (0, pad)).reshape(1, LANES)
    dskip_e = jnp.repeat(d_skip.astype(F32), HEAD_DIM).reshape(1, d_ssd)
    lane_head = lax.broadcasted_iota(jnp.int32, (LANES, d_ssd), 1) // HEAD_DIM
    row_head = lax.broadcasted_iota(jnp.int32, (LANES, d_ssd), 0)
    expand = jnp.where(lane_head == row_head, 1.0, 0.0).astype(BF16)
    chunk = lambda bi, ci: (bi, ci, 0)
    fixed = lambda bi, ci: (0, 0)
    return pl.pallas_call(
        functools.partial(_ssd_kernel, d_ssd=d_ssd),
        grid=(b, l // q),
        in_specs=[
            pl.BlockSpec((1, q, conv_dim), chunk),
            pl.BlockSpec((1, q, LANES), chunk),
            pl.BlockSpec((1, q, d_ssd), chunk),
            pl.BlockSpec((CONV_W, conv_dim), fixed),
            pl.BlockSpec((1, conv_dim), fixed),
            pl.BlockSpec((1, LANES), fixed),
            pl.BlockSpec((1, LANES), fixed),
            pl.BlockSpec((1, d_ssd), fixed),
            pl.BlockSpec((1, d_ssd), fixed),
            pl.BlockSpec((LANES, d_ssd), fixed),
        ],
        out_specs=pl.BlockSpec((1, q, d_ssd), chunk),
        out_shape=jax.ShapeDtypeStruct((b, l, d_ssd), BF16),
        scratch_shapes=[pltpu.VMEM((q + CONV_HALO, conv_dim), F32),
                        pltpu.VMEM((SSD_STATE, d_ssd), F32)],
        compiler_params=pltpu.CompilerParams(
            dimension_semantics=("arbitrary", "arbitrary"), vmem_limit_bytes=VMEM_LIMIT),
        name="ssd",
    )(xbc, dt_raw, z_ssd, conv_w.astype(F32), conv_b.astype(F32).reshape(1, conv_dim),
      dtb, alog, dskip_e, ssd_norm_w.astype(F32).reshape(1, d_ssd), expand)


def _out_proj_kernel(o_ref, z_ref, y_ref, x_ref, nw_ref, w_ref, out_ref):
    d_sb = o_ref.shape[-1]
    gated = o_ref[...].astype(F32) * _silu(z_ref[...].astype(F32))
    ms = jnp.mean(gated * gated, axis=-1, keepdims=True)
    y_sb = (gated * lax.rsqrt(ms + EPS) * nw_ref[...]).astype(BF16)
    out_ref[...] = (x_ref[...]
                    + jnp.dot(y_sb, w_ref[:d_sb, :], preferred_element_type=F32)
                    + jnp.dot(y_ref[...], w_ref[d_sb:, :], preferred_element_type=F32))


def _out_proj(o_sb, z_sb, y_ssd, x2, sb_norm_w, w_out):
    m, d = x2.shape
    d_sb = o_sb.shape[-1]
    d_ssd = y_ssd.shape[-1]
    row = lambda i: (i, 0)
    fixed = lambda i: (0, 0)
    return pl.pallas_call(
        _out_proj_kernel,
        grid=(m // ROW_TILE,),
        in_specs=[
            pl.BlockSpec((ROW_TILE, d_sb), row),
            pl.BlockSpec((ROW_TILE, d_sb), row),
            pl.BlockSpec((ROW_TILE, d_ssd), row),
            pl.BlockSpec((ROW_TILE, d), row),
            pl.BlockSpec((1, d_sb), fixed),
            pl.BlockSpec(w_out.shape, fixed, pipeline_mode=pl.Buffered(1)),
        ],
        out_specs=pl.BlockSpec((ROW_TILE, d), row),
        out_shape=jax.ShapeDtypeStruct((m, d), F32),
        compiler_params=pltpu.CompilerParams(
            dimension_semantics=("arbitrary",), vmem_limit_bytes=VMEM_LIMIT),
        name="out_proj",
    )(o_sb, z_sb, y_ssd, x2, sb_norm_w.astype(F32).reshape(1, d_sb), w_out)


def _layer(x, norm_w, w_in, q_norm_w, k_norm_w, conv_w, conv_b, dt_bias, a_log, d_skip,
           sb_norm_w, ssd_norm_w, w_out):
    b, l, d = x.shape
    d_sb = sb_norm_w.shape[0]
    d_ssd = ssd_norm_w.shape[0]
    conv_dim = conv_w.shape[1]
    heads = dt_bias.shape[0]
    main = 4 * d_sb + d_ssd + conv_dim
    assert w_in.shape[1] == main + heads and heads <= LANES
    assert conv_dim == d_ssd + 2 * SSD_GROUPS * SSD_STATE and heads * HEAD_DIM == d_ssd
    m = b * l
    x2 = x.reshape(m, d)
    w_main = w_in[:, :main].astype(BF16)
    w_dt = jnp.pad(w_in[:, main:], ((0, 0), (0, LANES - heads))).astype(BF16)
    q, k, v, z_sb, z_ssd, xbc, dt_raw = _in_proj(
        x2, norm_w.astype(F32), w_main, w_dt, d_sb, d_ssd, conv_dim)
    o_sb = _sb_attention(q.reshape(b, l, d_sb), k.reshape(b, l, d_sb), v.reshape(b, l, d_sb),
                         q_norm_w, k_norm_w)
    y_ssd = _ssd(xbc.reshape(b, l, conv_dim), dt_raw.reshape(b, l, LANES),
                 z_ssd.reshape(b, l, d_ssd), conv_w, conv_b, dt_bias, a_log, d_skip,
                 ssd_norm_w, d_ssd)
    out = _out_proj(o_sb.reshape(m, d_sb), z_sb, y_ssd.reshape(m, d_ssd), x2, sb_norm_w,
                    w_out.astype(BF16))
    return out.reshape(b, l, d)


def kernel(x, norm_w, w_in, q_norm_w, k_norm_w, conv_w, conv_b, dt_bias, A_log, D_skip,
           sb_norm_w, ssd_norm_w, w_out):
    for layer in range(norm_w.shape[0]):
        x = _layer(x, norm_w[layer], w_in[layer], q_norm_w[layer], k_norm_w[layer],
                   conv_w[layer], conv_b[layer], dt_bias[layer], A_log[layer], D_skip[layer],
                   sb_norm_w[layer], ssd_norm_w[layer], w_out[layer])
    return x
```

```python
import functools
import math

import jax
import jax.numpy as jnp
from jax import lax
from jax.experimental import pallas as pl
from jax.experimental.pallas import tpu as pltpu

F32 = jnp.float32
BF16 = jnp.bfloat16

EPS = 1e-6
LANES = 128
HEAD_DIM = 64
SSD_STATE = 128
SSD_GROUPS = 2
SSD_CHUNK = 128
CONV_W = 4
CONV_HALO = 8

ROW_TILE = 512
OUT_SUBTILES = 2
ATT_TILE = 128
ATT_WINDOW = 256
ATT_UNROLL = 8
KEY_NORM_ROWS = 1024
VMEM_LIMIT = 56 * 1024 * 1024
LOG2E = 1.4426950408889634
LOG2_STICK_FLOOR = -150.1
MASKED_LOG = -1e30


def _silu(z):
    return z * (1.0 / (1.0 + jnp.exp(-z)))


def _softplus(z):
    return jnp.maximum(z, 0.0) + jnp.log(1.0 + jnp.exp(-jnp.abs(z)))


def _in_proj_kernel(x_ref, nw_ref, w_ref, wdt_ref,
                    q_ref, k_ref, v_ref, zsb_ref, zssd_ref, xbc_ref, dt_ref):
    x = x_ref[...]
    ms = jnp.mean(x * x, axis=-1, keepdims=True)
    hn = (x * lax.rsqrt(ms + EPS) * nw_ref[...]).astype(BF16)
    start = 0
    for ref in (q_ref, k_ref, v_ref, zsb_ref, zssd_ref, xbc_ref):
        width = ref.shape[-1]
        ref[...] = jnp.dot(hn, w_ref[:, start:start + width],
                           preferred_element_type=F32).astype(ref.dtype)
        start += width
    dt_ref[...] = jnp.dot(hn, wdt_ref[...], preferred_element_type=F32)


def _in_proj(x2, norm_w, w_main, w_dt, d_sb, d_ssd, conv_dim):
    m, d = x2.shape
    widths = (d_sb, d_sb, d_sb, d_sb, d_ssd, conv_dim)
    row = lambda i: (i, 0)
    fixed = lambda i: (0, 0)
    out_shape = [jax.ShapeDtypeStruct((m, w), BF16) for w in widths]
    out_shape.append(jax.ShapeDtypeStruct((m, LANES), F32))
    out_specs = [pl.BlockSpec((ROW_TILE, w), row) for w in widths]
    out_specs.append(pl.BlockSpec((ROW_TILE, LANES), row))
    return pl.pallas_call(
        _in_proj_kernel,
        grid=(m // ROW_TILE,),
        in_specs=[
            pl.BlockSpec((ROW_TILE, d), row),
            pl.BlockSpec((1, d), fixed),
            pl.BlockSpec(w_main.shape, fixed, pipeline_mode=pl.Buffered(1)),
            pl.BlockSpec(w_dt.shape, fixed, pipeline_mode=pl.Buffered(1)),
        ],
        out_specs=out_specs,
        out_shape=out_shape,
        compiler_params=pltpu.CompilerParams(
            dimension_semantics=("arbitrary",), vmem_limit_bytes=VMEM_LIMIT),
        name="in_proj",
    )(x2, norm_w.reshape(1, d), w_main, w_dt)


def _attn_kernel(q_ref, k_ref, v_ref, qnw_ref, knw_ref, u_ref, o_ref, kn_ref, *, seq_len):
    tq, win = ATT_TILE, ATT_WINDOW
    first = lax.broadcasted_iota(jnp.int32, (1, LANES), 1) < HEAD_DIM
    col_minus_row = (lax.broadcasted_iota(jnp.int32, (tq, win), 1)
                     - lax.broadcasted_iota(jnp.int32, (tq, win), 0))

    same_head = (lax.broadcasted_iota(jnp.int32, (LANES, LANES), 0) < HEAD_DIM) == first
    head_ones = jnp.where(same_head, 1.0, 0.0).astype(BF16)

    def head_norm(t, w):
        ss = jnp.dot((t * t).astype(BF16), head_ones, preferred_element_type=F32)
        return t * lax.rsqrt(ss * (1.0 / HEAD_DIM) + EPS) * w

    def norm_keys(i, carry):
        r = pl.multiple_of(i * KEY_NORM_ROWS, KEY_NORM_ROWS)
        kk = k_ref[0, pl.ds(r, KEY_NORM_ROWS), :].astype(F32)
        kn_ref[pl.ds(r, KEY_NORM_ROWS), :] = head_norm(kk, knw_ref[...]).astype(BF16)
        return carry

    lax.fori_loop(0, seq_len // KEY_NORM_ROWS, norm_keys, 0)

    def stage_scores(qm, start, width):
        kb = kn_ref[pl.ds(pl.multiple_of(start, tq), width), :]
        return lax.dot_general(qm, kb, (((1,), (1,)), ((), ())), preferred_element_type=F32)

    def stage_log_terms(z, mask):
        sp = jnp.maximum(z, 0.0) + jnp.log(1.0 + jnp.exp2(-jnp.abs(z))) * LOG2E
        lq = z - sp
        if mask is not None:
            sp = jnp.where(mask, sp, 0.0)
        return sp.astype(BF16), lq, jnp.sum(sp, axis=1, keepdims=True)

    def stage_cumsum(sp_bf):
        width = sp_bf.shape[1]
        return jnp.dot(sp_bf, u_ref[:width, :width], preferred_element_type=F32)

    def stage_weights(lq, rl, mask):
        w = jnp.exp2(lq + rl)
        if mask is not None:
            w = jnp.where(mask, w, 0.0)
        return w.astype(BF16)

    def stage_values(w_bf, start):
        vb = v_ref[0, pl.ds(pl.multiple_of(start, tq), w_bf.shape[1]), :]
        return jnp.dot(w_bf, vb, preferred_element_type=F32)

    def key_window(qm, start, width):
        sp_bf, lq, s = stage_log_terms(stage_scores(qm, start, width), None)
        w_bf = stage_weights(lq, stage_cumsum(sp_bf), None)
        return stage_values(w_bf, start), s

    def log_stick_left(cs):
        return jnp.max(jnp.maximum(cs[0], cs[1]))

    def query_tiles(it, carry):
        pairs = []
        for b in range(ATT_UNROLL):
            t = it * ATT_UNROLL + b
            row0 = t * tq
            qn = head_norm(q_ref[0, pl.ds(pl.multiple_of(row0, tq), tq), :].astype(F32),
                           qnw_ref[...])
            qn = qn * (LOG2E / math.sqrt(HEAD_DIM))
            start1 = jnp.maximum(t - 1, 0) * tq
            start2 = jnp.maximum(t - 2, 0) * tq
            mask = col_minus_row < row0 - start1
            for h in range(2):
                qm = jnp.where(first if h == 0 else jnp.logical_not(first), qn, 0.0).astype(BF16)
                pairs.append({"t": t, "qm": qm,
                              "wins": ((start1, win, mask), (start2, tq, None))})
        for step in range(len(pairs) + 4):
            for lag, p in enumerate(pairs):
                stage = step - lag
                if stage == 0:
                    p["z"] = [stage_scores(p["qm"], s, n) for s, n, _ in p["wins"]]
                elif stage == 1:
                    p["log"] = [stage_log_terms(z, w[2]) for z, w in zip(p["z"], p["wins"])]
                elif stage == 2:
                    p["rl"] = [stage_cumsum(t[0]) for t in p["log"]]
                elif stage == 3:
                    p["w"] = [stage_weights(t[1], rl, w[2])
                              for t, rl, w in zip(p["log"], p["rl"], p["wins"])]
                elif stage == 4:
                    p["pv"] = [stage_values(w, win_[0]) for w, win_ in zip(p["w"], p["wins"])]

        starts = []
        for b in range(ATT_UNROLL):
            t = pairs[2 * b]["t"]
            has_second = t >= 2
            accs, cs = [], []
            for p in pairs[2 * b:2 * b + 2]:
                c = -p["log"][0][2]
                accs.append(p["pv"][0] + jnp.where(has_second, jnp.exp2(c) * p["pv"][1], 0.0))
                cs.append(c - jnp.where(has_second, p["log"][1][2], 0.0))
            starts.append((t - 3, log_stick_left(cs), accs[0], accs[1], cs[0], cs[1]))

        def cond(state):
            return jnp.logical_and(state[0] >= 0, state[1] > LOG2_STICK_FLOOR)

        for b in range(ATT_UNROLL):
            qms = [pairs[2 * b]["qm"], pairs[2 * b + 1]["qm"]]

            def body(state, qms=qms):
                j, _, a0, a1, c0, c1 = state
                accs, cs = [], []
                for h, (a, c) in enumerate(((a0, c0), (a1, c1))):
                    pv, s = key_window(qms[h], j * tq, tq)
                    accs.append(a + jnp.exp2(c) * pv)
                    cs.append(c - s)
                return (j - 1, log_stick_left(cs), accs[0], accs[1], cs[0], cs[1])

            state = lax.while_loop(cond, body, starts[b])
            row0 = pl.multiple_of(pairs[2 * b]["t"] * tq, tq)
            o_ref[0, pl.ds(row0, tq), :] = (
                jnp.where(first, state[2], state[3]).astype(o_ref.dtype))
        return carry

    lax.fori_loop(0, seq_len // (tq * ATT_UNROLL), query_tiles, 0)


def _sb_attention(q, k, v, q_norm_w, k_norm_w):
    b, l, d = q.shape
    blk = ATT_WINDOW
    r_id = lax.broadcasted_iota(jnp.int32, (blk, blk), 0)
    c_id = lax.broadcasted_iota(jnp.int32, (blk, blk), 1)
    u_neg = jnp.where(r_id > c_id, -1.0, 0.0).astype(BF16)
    qnw = jnp.tile(q_norm_w.astype(F32), LANES // HEAD_DIM).reshape(1, LANES)
    knw = jnp.tile(k_norm_w.astype(F32), LANES // HEAD_DIM).reshape(1, LANES)
    pair = lambda bi, p: (bi, 0, p)
    fixed = lambda bi, p: (0, 0)
    return pl.pallas_call(
        functools.partial(_attn_kernel, seq_len=l),
        grid=(b, d // LANES),
        in_specs=[
            pl.BlockSpec((1, l, LANES), pair),
            pl.BlockSpec((1, l, LANES), pair),
            pl.BlockSpec((1, l, LANES), pair),
            pl.BlockSpec((1, LANES), fixed),
            pl.BlockSpec((1, LANES), fixed),
            pl.BlockSpec((blk, blk), fixed),
        ],
        out_specs=pl.BlockSpec((1, l, LANES), pair),
        out_shape=jax.ShapeDtypeStruct((b, l, d), BF16),
        scratch_shapes=[pltpu.VMEM((l, LANES), BF16)],
        compiler_params=pltpu.CompilerParams(
            dimension_semantics=("arbitrary", "arbitrary"), vmem_limit_bytes=VMEM_LIMIT),
        name="sb_attn",
    )(q, k, v, qnw, knw, u_neg)


def _ssd_kernel(xbc_ref, dt_ref, z_ref, cw_ref, cb_ref, dtb_ref, alog_ref, dskip_ref, nw_ref,
                expand_ref, y_ref, ext_ref, state_ref, *, d_ssd):
    ci = pl.program_id(1)
    q = SSD_CHUNK
    gw = SSD_STATE
    heads_per_group = d_ssd // HEAD_DIM // SSD_GROUPS
    group_w = heads_per_group * HEAD_DIM

    @pl.when(ci == 0)
    def _():
        ext_ref[0:CONV_HALO, :] = jnp.zeros((CONV_HALO, ext_ref.shape[1]), F32)
        state_ref[...] = jnp.zeros(state_ref.shape, F32)

    @pl.when(ci > 0)
    def _():
        ext_ref[0:CONV_HALO, :] = ext_ref[q:q + CONV_HALO, :]

    ext_ref[CONV_HALO:CONV_HALO + q, :] = xbc_ref[0].astype(F32)

    ext = ext_ref[...]
    conv = cb_ref[...] + ext[CONV_HALO:, :] * cw_ref[CONV_W - 1:CONV_W, :]
    for back in range(1, CONV_W):
        tap = CONV_W - 1 - back
        conv = conv + pltpu.roll(ext, back, axis=0)[CONV_HALO:, :] * cw_ref[tap:tap + 1, :]
    u = _silu(conv)
    xs = u[:, :d_ssd]
    b_all = u[:, d_ssd:d_ssd + SSD_GROUPS * gw]
    c_all = u[:, d_ssd + SSD_GROUPS * gw:]

    dt = _softplus(dt_ref[0] + dtb_ref[...])
    d_a = dt * (-jnp.exp(alog_ref[...]))
    r_id = lax.broadcasted_iota(jnp.int32, (q, q), 0)
    c_id = lax.broadcasted_iota(jnp.int32, (q, q), 1)
    lower = c_id <= r_id
    tri = jnp.where(lower, 1.0, 0.0).astype(BF16)
    d_a_hi = d_a.astype(BF16)
    d_a_lo = (d_a - d_a_hi.astype(F32)).astype(BF16)
    a_cs = (jnp.dot(tri, d_a_hi, preferred_element_type=F32)
            + jnp.dot(tri, d_a_lo, preferred_element_type=F32))
    a_cs_t = a_cs.T
    a_last = a_cs[q - 1:q, :]

    def expand(t):
        return jnp.dot(t.astype(BF16), expand_ref[...], preferred_element_type=F32)

    dt_e = expand(dt)
    fac_e = expand(dt * jnp.exp(a_last - a_cs))
    ea_e = expand(jnp.exp(a_cs))

    xdt = xs * dt_e
    xfac = (xs * fac_e).astype(BF16)
    first = lax.broadcasted_iota(jnp.int32, (1, LANES), 1) < HEAD_DIM

    y_parts = []
    new_states = []
    for g in range(SSD_GROUPS):
        bg = b_all[:, g * gw:(g + 1) * gw]
        cg = c_all[:, g * gw:(g + 1) * gw].astype(BF16)
        cb = lax.dot_general(cg, bg.astype(BF16), (((1,), (1,)), ((), ())),
                             preferred_element_type=F32)
        bg_t = bg.T.astype(BF16)
        lo, hi = g * group_w, (g + 1) * group_w
        st_old = state_ref[:, lo:hi]
        y_off = jnp.dot(cg, st_old.astype(BF16), preferred_element_type=F32) * ea_e[:, lo:hi]
        new_states.append(st_old * ea_e[q - 1:q, lo:hi]
                          + jnp.dot(bg_t, xfac[:, lo:hi], preferred_element_type=F32))
        for m in range(heads_per_group // 2):
            base = lo + m * LANES
            x_pair = xdt[:, base:base + LANES]
            y_pair = None
            for h in range(2):
                kh = base // HEAD_DIM + h
                seg = a_cs[:, kh:kh + 1] - a_cs_t[kh:kh + 1, :]
                decay = jnp.exp(jnp.where(lower, seg, MASKED_LOG))
                wts = (cb * decay).astype(BF16)
                x_h = jnp.where(first if h == 0 else jnp.logical_not(first), x_pair, 0.0)
                part = jnp.dot(wts, x_h.astype(BF16), preferred_element_type=F32)
                y_pair = part if y_pair is None else y_pair + part
            y_parts.append(y_pair + y_off[:, m * LANES:(m + 1) * LANES])
    for g in range(SSD_GROUPS):
        state_ref[:, g * group_w:(g + 1) * group_w] = new_states[g]

    y = jnp.concatenate(y_parts, axis=1) + dskip_ref[...] * xs
    gated = y * _silu(z_ref[0].astype(F32))
    ms = jnp.mean(gated * gated, axis=-1, keepdims=True)
    y_ref[0] = (gated * lax.rsqrt(ms + EPS) * nw_ref[...]).astype(y_ref.dtype)


def _ssd(xbc, dt_raw, z_ssd, conv_w, conv_b, dt_bias, a_log, d_skip, ssd_norm_w, d_ssd):
    b, l, conv_dim = xbc.shape
    heads = d_ssd // HEAD_DIM
    q = SSD_CHUNK
    pad = LANES - heads
    dtb = jnp.pad(dt_bias.astype(F32), (0, pad)).reshape(1, LANES)
    alog = jnp.pad(a_log.astype(F32), (0, pad)).reshape(1, LANES)
    dskip_e = jnp.repeat(d_skip.astype(F32), HEAD_DIM).reshape(1, d_ssd)
    lane_head = lax.broadcasted_iota(jnp.int32, (LANES, d_ssd), 1) // HEAD_DIM
    row_head = lax.broadcasted_iota(jnp.int32, (LANES, d_ssd), 0)
    expand = jnp.where(lane_head == row_head, 1.0, 0.0).astype(BF16)
    chunk = lambda bi, ci: (bi, ci, 0)
    fixed = lambda bi, ci: (0, 0)
    return pl.pallas_call(
        functools.partial(_ssd_kernel, d_ssd=d_ssd),
        grid=(b, l // q),
        in_specs=[
            pl.BlockSpec((1, q, conv_dim), chunk),
            pl.BlockSpec((1, q, LANES), chunk),
            pl.BlockSpec((1, q, d_ssd), chunk),
            pl.BlockSpec((CONV_W, conv_dim), fixed),
            pl.BlockSpec((1, conv_dim), fixed),
            pl.BlockSpec((1, LANES), fixed),
            pl.BlockSpec((1, LANES), fixed),
            pl.BlockSpec((1, d_ssd), fixed),
            pl.BlockSpec((1, d_ssd), fixed),
            pl.BlockSpec((LANES, d_ssd), fixed),
        ],
        out_specs=pl.BlockSpec((1, q, d_ssd), chunk),
        out_shape=jax.ShapeDtypeStruct((b, l, d_ssd), BF16),
        scratch_shapes=[pltpu.VMEM((q + CONV_HALO, conv_dim), F32),
                        pltpu.VMEM((SSD_STATE, d_ssd), F32)],
        compiler_params=pltpu.CompilerParams(
            dimension_semantics=("arbitrary", "arbitrary"), vmem_limit_bytes=VMEM_LIMIT),
        name="ssd",
    )(xbc, dt_raw, z_ssd, conv_w.astype(F32), conv_b.astype(F32).reshape(1, conv_dim),
      dtb, alog, dskip_e, ssd_norm_w.astype(F32).reshape(1, d_ssd), expand)


def _out_proj_kernel(o_ref, z_ref, y_ref, x_ref, nw_ref, w_ref, out_ref):
    d_sb = o_ref.shape[-1]
    rows = o_ref.shape[0] // OUT_SUBTILES
    accs, y_sbs = [], []
    for t in range(OUT_SUBTILES + 1):
        if t < OUT_SUBTILES:
            sl = pl.ds(t * rows, rows)
            accs.append(x_ref[sl, :]
                        + jnp.dot(y_ref[sl, :], w_ref[d_sb:, :], preferred_element_type=F32))
            gated = o_ref[sl, :].astype(F32) * _silu(z_ref[sl, :].astype(F32))
            ms = jnp.mean(gated * gated, axis=-1, keepdims=True)
            y_sbs.append((gated * lax.rsqrt(ms + EPS) * nw_ref[...]).astype(BF16))
        if t > 0:
            out_ref[pl.ds((t - 1) * rows, rows), :] = accs[t - 1] + jnp.dot(
                y_sbs[t - 1], w_ref[:d_sb, :], preferred_element_type=F32)


def _out_proj(o_sb, z_sb, y_ssd, x2, sb_norm_w, w_out):
    m, d = x2.shape
    d_sb = o_sb.shape[-1]
    d_ssd = y_ssd.shape[-1]
    row = lambda i: (i, 0)
    fixed = lambda i: (0, 0)
    return pl.pallas_call(
        _out_proj_kernel,
        grid=(m // ROW_TILE,),
        in_specs=[
            pl.BlockSpec((ROW_TILE, d_sb), row),
            pl.BlockSpec((ROW_TILE, d_sb), row),
            pl.BlockSpec((ROW_TILE, d_ssd), row),
            pl.BlockSpec((ROW_TILE, d), row),
            pl.BlockSpec((1, d_sb), fixed),
            pl.BlockSpec(w_out.shape, fixed, pipeline_mode=pl.Buffered(1)),
        ],
        out_specs=pl.BlockSpec((ROW_TILE, d), row),
        out_shape=jax.ShapeDtypeStruct((m, d), F32),
        compiler_params=pltpu.CompilerParams(
            dimension_semantics=("arbitrary",), vmem_limit_bytes=VMEM_LIMIT),
        name="out_proj",
    )(o_sb, z_sb, y_ssd, x2, sb_norm_w.astype(F32).reshape(1, d_sb), w_out)


def _layer(x, norm_w, w_in, q_norm_w, k_norm_w, conv_w, conv_b, dt_bias, a_log, d_skip,
           sb_norm_w, ssd_norm_w, w_out):
    b, l, d = x.shape
    d_sb = sb_norm_w.shape[0]
    d_ssd = ssd_norm_w.shape[0]
    conv_dim = conv_w.shape[1]
    heads = dt_bias.shape[0]
    main = 4 * d_sb + d_ssd + conv_dim
    assert w_in.shape[1] == main + heads and heads <= LANES
    assert conv_dim == d_ssd + 2 * SSD_GROUPS * SSD_STATE and heads * HEAD_DIM == d_ssd
    m = b * l
    x2 = x.reshape(m, d)
    w_main = w_in[:, :main].astype(BF16)
    w_dt = jnp.pad(w_in[:, main:], ((0, 0), (0, LANES - heads))).astype(BF16)
    q, k, v, z_sb, z_ssd, xbc, dt_raw = _in_proj(
        x2, norm_w.astype(F32), w_main, w_dt, d_sb, d_ssd, conv_dim)
    o_sb = _sb_attention(q.reshape(b, l, d_sb), k.reshape(b, l, d_sb), v.reshape(b, l, d_sb),
                         q_norm_w, k_norm_w)
    y_ssd = _ssd(xbc.reshape(b, l, conv_dim), dt_raw.reshape(b, l, LANES),
                 z_ssd.reshape(b, l, d_ssd), conv_w, conv_b, dt_bias, a_log, d_skip,
                 ssd_norm_w, d_ssd)
    out = _out_proj(o_sb.reshape(m, d_sb), z_sb, y_ssd.reshape(m, d_ssd), x2, sb_norm_w,
                    w_out.astype(BF16))
    return out.reshape(b, l, d)


def kernel(x, norm_w, w_in, q_norm_w, k_norm_w, conv_w, conv_b, dt_bias, A_log, D_skip,
           sb_norm_w, ssd_norm_w, w_out):
    for layer in range(norm_w.shape[0]):
        x = _layer(x, norm_w[layer], w_in[layer], q_norm_w[layer], k_norm_w[layer],
                   conv_w[layer], conv_b[layer], dt_bias[layer], A_log[layer], D_skip[layer],
                   sb_norm_w[layer], ssd_norm_w[layer], w_out[layer])
    return x
```

```python
import functools
import math

import jax
import jax.numpy as jnp
from jax import lax
from jax.experimental import pallas as pl
from jax.experimental.pallas import tpu as pltpu

F32 = jnp.float32
BF16 = jnp.bfloat16

EPS = 1e-6
LANES = 128
HEAD_DIM = 64
SSD_STATE = 128
SSD_GROUPS = 2
SSD_CHUNK = 128
SSD_STEP_CHUNKS = 4
CONV_W = 4
CONV_HALO = 8

ROW_TILE = 512
PROJ_COLS = 256
CONV_ROWS = 128
ATT_TILE = 128
ATT_WINDOW = 256
ATT_UNROLL = 8
KEY_NORM_ROWS = 2048
VMEM_LIMIT = 56 * 1024 * 1024
LOG2E = 1.4426950408889634
LOG2_STICK_FLOOR = -150.1
MASKED_LOG = -1e30


def _silu(z):
    return z * (1.0 / (1.0 + jnp.exp(-z)))


def _softplus(z):
    return jnp.maximum(z, 0.0) + jnp.log(1.0 + jnp.exp(-jnp.abs(z)))


def _in_proj_kernel(x_ref, nw_ref, w_ref, wdt_ref, cw_ref, cb_ref,
                    q_ref, k_ref, v_ref, gsb_ref, gssd_ref, u_ref, dt_ref, ext_ref,
                    *, tiles_per_seq):
    rows = x_ref.shape[0]
    x = x_ref[...]
    ms = jnp.mean(x * x, axis=-1, keepdims=True)
    hn = (x * lax.rsqrt(ms + EPS) * nw_ref[...]).astype(BF16)

    def proj(col, width):
        return jnp.dot(hn, w_ref[:, col:col + width], preferred_element_type=F32)

    d_sb, d_ssd, conv_dim = q_ref.shape[-1], gssd_ref.shape[-1], u_ref.shape[-1]
    col_xbc = 4 * d_sb + d_ssd

    @pl.when(pl.program_id(0) % tiles_per_seq == 0)
    def _():
        ext_ref[0:CONV_HALO, :] = jnp.zeros((CONV_HALO, conv_dim), F32)

    @pl.when(pl.program_id(0) % tiles_per_seq != 0)
    def _():
        ext_ref[0:CONV_HALO, :] = ext_ref[rows:rows + CONV_HALO, :]

    def conv_tile(n, anchor):
        r0 = (n % row_tiles) * CONV_ROWS
        cols = pl.ds((n // row_tiles) * PROJ_COLS, PROJ_COLS)
        ext = ext_ref[pl.ds(r0, CONV_ROWS + CONV_HALO), cols]
        conv = (cb_ref[:, cols] + anchor) + ext[CONV_HALO:, :] * cw_ref[CONV_W - 1:CONV_W, cols]
        for back in range(1, CONV_W):
            tap = CONV_W - 1 - back
            conv = conv + (pltpu.roll(ext, back, axis=0)[CONV_HALO:, :]
                           * cw_ref[tap:tap + 1, cols])
        u_ref[pl.ds(r0, CONV_ROWS), cols] = _silu(conv).astype(u_ref.dtype)

    ext_ref[CONV_HALO:, :] = proj(col_xbc, conv_dim)
    row_tiles = rows // CONV_ROWS
    conv_tiles = row_tiles * (conv_dim // PROJ_COLS)
    chunks = [(ref, col + c, c, gate)
              for ref, col, gate in ((q_ref, 0, False), (k_ref, d_sb, False),
                                     (v_ref, 2 * d_sb, False), (gsb_ref, 3 * d_sb, True),
                                     (gssd_ref, 4 * d_sb, True))
              for c in range(0, ref.shape[-1], PROJ_COLS)]
    per_chunk = -(-conv_tiles // len(chunks))
    n = 0
    for ref, w_col, c, gate in chunks:
        p = proj(w_col, PROJ_COLS)
        ref[:, c:c + PROJ_COLS] = (_silu(p) if gate else p).astype(ref.dtype)
        anchor = p[rows - 1:rows, :] * 0.0
        for _ in range(per_chunk):
            if n < conv_tiles:
                conv_tile(n, anchor)
            n += 1
    dt_ref[...] = jnp.dot(hn, wdt_ref[...], preferred_element_type=F32)


def _in_proj(x2, norm_w, w_main, w_dt, conv_w, conv_b, d_sb, d_ssd, seq_len):
    m, d = x2.shape
    conv_dim = conv_w.shape[1]
    widths = (d_sb, d_sb, d_sb, d_sb, d_ssd, conv_dim)
    row = lambda i: (i, 0)
    fixed = lambda i: (0, 0)
    out_shape = [jax.ShapeDtypeStruct((m, w), BF16) for w in widths]
    out_shape.append(jax.ShapeDtypeStruct((m, LANES), F32))
    out_specs = [pl.BlockSpec((ROW_TILE, w), row) for w in widths]
    out_specs.append(pl.BlockSpec((ROW_TILE, LANES), row))
    return pl.pallas_call(
        functools.partial(_in_proj_kernel, tiles_per_seq=seq_len // ROW_TILE),
        grid=(m // ROW_TILE,),
        in_specs=[
            pl.BlockSpec((ROW_TILE, d), row),
            pl.BlockSpec((1, d), fixed),
            pl.BlockSpec(w_main.shape, fixed, pipeline_mode=pl.Buffered(1)),
            pl.BlockSpec(w_dt.shape, fixed, pipeline_mode=pl.Buffered(1)),
            pl.BlockSpec((CONV_W, conv_dim), fixed),
            pl.BlockSpec((1, conv_dim), fixed),
        ],
        out_specs=out_specs,
        out_shape=out_shape,
        scratch_shapes=[pltpu.VMEM((ROW_TILE + CONV_HALO, conv_dim), F32)],
        compiler_params=pltpu.CompilerParams(
            dimension_semantics=("arbitrary",), vmem_limit_bytes=VMEM_LIMIT),
        name="in_proj",
    )(x2, norm_w.reshape(1, d), w_main, w_dt, conv_w.astype(F32),
      conv_b.astype(F32).reshape(1, conv_dim))


def _attn_kernel(q_ref, k_ref, v_ref, qnw_ref, knw_ref, u_ref, o_ref, kn_ref, *, seq_len):
    tq, win = ATT_TILE, ATT_WINDOW
    first = lax.broadcasted_iota(jnp.int32, (1, LANES), 1) < HEAD_DIM
    col_minus_row = (lax.broadcasted_iota(jnp.int32, (tq, win), 1)
                     - lax.broadcasted_iota(jnp.int32, (tq, win), 0))

    same_head = (lax.broadcasted_iota(jnp.int32, (LANES, LANES), 0) < HEAD_DIM) == first
    head_ones = jnp.where(same_head, 1.0, 0.0).astype(BF16)

    def head_norm(t, w):
        ss = jnp.dot((t * t).astype(BF16), head_ones, preferred_element_type=F32)
        return t * lax.rsqrt(ss * (1.0 / HEAD_DIM) + EPS) * w

    def norm_keys(i, carry):
        r = pl.multiple_of(i * KEY_NORM_ROWS, KEY_NORM_ROWS)
        kk = k_ref[0, pl.ds(r, KEY_NORM_ROWS), :].astype(F32)
        kn_ref[pl.ds(r, KEY_NORM_ROWS), :] = head_norm(kk, knw_ref[...]).astype(BF16)
        return carry

    lax.fori_loop(0, seq_len // KEY_NORM_ROWS, norm_keys, 0)

    def stage_scores(qm, start, width):
        kb = kn_ref[pl.ds(pl.multiple_of(start, tq), width), :]
        return lax.dot_general(qm, kb, (((1,), (1,)), ((), ())), preferred_element_type=F32)

    def stage_log_terms(z, mask):
        sp = jnp.maximum(z, 0.0) + jnp.log(1.0 + jnp.exp2(-jnp.abs(z))) * LOG2E
        lq = z - sp
        if mask is not None:
            sp = jnp.where(mask, sp, 0.0)
        return sp.astype(BF16), lq, jnp.sum(sp, axis=1, keepdims=True)

    def stage_cumsum(sp_bf):
        width = sp_bf.shape[1]
        return jnp.dot(sp_bf, u_ref[:width, :width], preferred_element_type=F32)

    def stage_weights(lq, rl, mask):
        w = jnp.exp2(lq + rl)
        if mask is not None:
            w = jnp.where(mask, w, 0.0)
        return w.astype(BF16)

    def stage_values(w_bf, start):
        vb = v_ref[0, pl.ds(pl.multiple_of(start, tq), w_bf.shape[1]), :]
        return jnp.dot(w_bf, vb, preferred_element_type=F32)

    def key_window(qm, start, width):
        sp_bf, lq, s = stage_log_terms(stage_scores(qm, start, width), None)
        w_bf = stage_weights(lq, stage_cumsum(sp_bf), None)
        return stage_values(w_bf, start), s

    def log_stick_left(cs):
        return jnp.max(jnp.maximum(cs[0], cs[1]))

    def query_tiles(it, carry):
        pairs = []
        for b in range(ATT_UNROLL):
            t = it * ATT_UNROLL + b
            row0 = t * tq
            qn = head_norm(q_ref[0, pl.ds(pl.multiple_of(row0, tq), tq), :].astype(F32),
                           qnw_ref[...])
            qn = qn * (LOG2E / math.sqrt(HEAD_DIM))
            start1 = jnp.maximum(t - 1, 0) * tq
            start2 = jnp.maximum(t - 2, 0) * tq
            mask = col_minus_row < row0 - start1
            for h in range(2):
                qm = jnp.where(first if h == 0 else jnp.logical_not(first), qn, 0.0).astype(BF16)
                pairs.append({"t": t, "qm": qm,
                              "wins": ((start1, win, mask), (start2, tq, None))})
        for step in range(len(pairs) + 4):
            for lag, p in enumerate(pairs):
                stage = step - lag
                if stage == 0:
                    p["z"] = [stage_scores(p["qm"], s, n) for s, n, _ in p["wins"]]
                elif stage == 1:
                    p["log"] = [stage_log_terms(z, w[2]) for z, w in zip(p["z"], p["wins"])]
                elif stage == 2:
                    p["rl"] = [stage_cumsum(t[0]) for t in p["log"]]
                elif stage == 3:
                    p["w"] = [stage_weights(t[1], rl, w[2])
                              for t, rl, w in zip(p["log"], p["rl"], p["wins"])]
                elif stage == 4:
                    p["pv"] = [stage_values(w, win_[0]) for w, win_ in zip(p["w"], p["wins"])]

        starts = []
        for b in range(ATT_UNROLL):
            t = pairs[2 * b]["t"]
            has_second = t >= 2
            accs, cs = [], []
            for p in pairs[2 * b:2 * b + 2]:
                c = -p["log"][0][2]
                accs.append(p["pv"][0] + jnp.where(has_second, jnp.exp2(c) * p["pv"][1], 0.0))
                cs.append(c - jnp.where(has_second, p["log"][1][2], 0.0))
            starts.append((t - 3, log_stick_left(cs), accs[0], accs[1], cs[0], cs[1]))

        def cond(state):
            return jnp.logical_and(state[0] >= 0, state[1] > LOG2_STICK_FLOOR)

        for b in range(ATT_UNROLL):
            qms = [pairs[2 * b]["qm"], pairs[2 * b + 1]["qm"]]

            def body(state, qms=qms):
                j, _, a0, a1, c0, c1 = state
                accs, cs = [], []
                for h, (a, c) in enumerate(((a0, c0), (a1, c1))):
                    pv, s = key_window(qms[h], j * tq, tq)
                    accs.append(a + jnp.exp2(c) * pv)
                    cs.append(c - s)
                return (j - 1, log_stick_left(cs), accs[0], accs[1], cs[0], cs[1])

            state = lax.while_loop(cond, body, starts[b])
            row0 = pl.multiple_of(pairs[2 * b]["t"] * tq, tq)
            o_ref[0, pl.ds(row0, tq), :] = (
                jnp.where(first, state[2], state[3]).astype(o_ref.dtype))
        return carry

    lax.fori_loop(0, seq_len // (tq * ATT_UNROLL), query_tiles, 0)


def _sb_attention(q, k, v, q_norm_w, k_norm_w):
    b, l, d = q.shape
    assert l % KEY_NORM_ROWS == 0 and l % (ATT_TILE * ATT_UNROLL) == 0 and d % LANES == 0
    blk = ATT_WINDOW
    r_id = lax.broadcasted_iota(jnp.int32, (blk, blk), 0)
    c_id = lax.broadcasted_iota(jnp.int32, (blk, blk), 1)
    u_neg = jnp.where(r_id > c_id, -1.0, 0.0).astype(BF16)
    qnw = jnp.tile(q_norm_w.astype(F32), LANES // HEAD_DIM).reshape(1, LANES)
    knw = jnp.tile(k_norm_w.astype(F32), LANES // HEAD_DIM).reshape(1, LANES)
    pair = lambda bi, p: (bi, 0, p)
    fixed = lambda bi, p: (0, 0)
    return pl.pallas_call(
        functools.partial(_attn_kernel, seq_len=l),
        grid=(b, d // LANES),
        in_specs=[
            pl.BlockSpec((1, l, LANES), pair),
            pl.BlockSpec((1, l, LANES), pair),
            pl.BlockSpec((1, l, LANES), pair),
            pl.BlockSpec((1, LANES), fixed),
            pl.BlockSpec((1, LANES), fixed),
            pl.BlockSpec((blk, blk), fixed),
        ],
        out_specs=pl.BlockSpec((1, l, LANES), pair),
        out_shape=jax.ShapeDtypeStruct((b, l, d), BF16),
        scratch_shapes=[pltpu.VMEM((l, LANES), BF16)],
        compiler_params=pltpu.CompilerParams(
            dimension_semantics=("arbitrary", "arbitrary"), vmem_limit_bytes=VMEM_LIMIT),
        name="sb_attn",
    )(q, k, v, qnw, knw, u_neg)


def _ssd_kernel(u_ref, dt_ref, g_ref, dtb_ref, alog_ref, dskip_ref, nw_ref,
                expand_ref, y_ref, state_ref, *, d_ssd):
    ci = pl.program_id(1)
    q = SSD_CHUNK
    gw = SSD_STATE
    heads_per_group = d_ssd // HEAD_DIM // SSD_GROUPS
    group_w = heads_per_group * HEAD_DIM

    @pl.when(ci == 0)
    def _():
        state_ref[...] = jnp.zeros(state_ref.shape, F32)

    r_id = lax.broadcasted_iota(jnp.int32, (q, q), 0)
    c_id = lax.broadcasted_iota(jnp.int32, (q, q), 1)
    lower = c_id <= r_id
    tri = jnp.where(lower, 1.0, 0.0).astype(BF16)
    first = lax.broadcasted_iota(jnp.int32, (1, LANES), 1) < HEAD_DIM
    neg_a = -jnp.exp(alog_ref[...])

    def expand(t):
        return jnp.dot(t.astype(BF16), expand_ref[...], preferred_element_type=F32)

    def chunk_local(k):
        rows = pl.ds(k * q, q)
        xs = u_ref[0, rows, :d_ssd].astype(F32)
        b_all = u_ref[0, rows, d_ssd:d_ssd + SSD_GROUPS * gw]
        c_all = u_ref[0, rows, d_ssd + SSD_GROUPS * gw:]

        dt = _softplus(dt_ref[0, rows, :] + dtb_ref[...])
        d_a = dt * neg_a
        d_a_hi = d_a.astype(BF16)
        d_a_lo = (d_a - d_a_hi.astype(F32)).astype(BF16)
        a_cs = (jnp.dot(tri, d_a_hi, preferred_element_type=F32)
                + jnp.dot(tri, d_a_lo, preferred_element_type=F32))
        a_cs_t = a_cs.T
        a_last = a_cs[q - 1:q, :]
        dt_e = expand(dt)
        fac_e = expand(dt * jnp.exp(a_last - a_cs))
        ea_e = expand(jnp.exp(a_cs))
        xdt = xs * dt_e
        xfac = (xs * fac_e).astype(BF16)

        y_parts, cgs, d_states = [], [], []
        for g in range(SSD_GROUPS):
            bg = b_all[:, g * gw:(g + 1) * gw]
            cg = c_all[:, g * gw:(g + 1) * gw]
            cb = lax.dot_general(cg, bg, (((1,), (1,)), ((), ())),
                                 preferred_element_type=F32)
            bg_t = bg.astype(F32).T.astype(BF16)
            lo = g * group_w
            cgs.append(cg)
            d_states.append(jnp.dot(bg_t, xfac[:, lo:lo + group_w],
                                    preferred_element_type=F32))
            for m in range(heads_per_group // 2):
                base = lo + m * LANES
                x_pair = xdt[:, base:base + LANES]
                y_pair = None
                for h in range(2):
                    kh = base // HEAD_DIM + h
                    seg = a_cs[:, kh:kh + 1] - a_cs_t[kh:kh + 1, :]
                    decay = jnp.exp(jnp.where(lower, seg, MASKED_LOG))
                    wts = (cb * decay).astype(BF16)
                    x_h = jnp.where(first if h == 0 else jnp.logical_not(first), x_pair, 0.0)
                    part = jnp.dot(wts, x_h.astype(BF16), preferred_element_type=F32)
                    y_pair = part if y_pair is None else y_pair + part
                y_parts.append(y_pair)
        y_local = jnp.concatenate(y_parts, axis=1) + dskip_ref[...] * xs
        return y_local, cgs, d_states, ea_e

    local = [chunk_local(k) for k in range(SSD_STEP_CHUNKS)]

    states = [state_ref[:, g * group_w:(g + 1) * group_w] for g in range(SSD_GROUPS)]
    for k, (y_local, cgs, d_states, ea_e) in enumerate(local):
        y_off = jnp.concatenate(
            [jnp.dot(cgs[g], states[g].astype(BF16), preferred_element_type=F32)
             for g in range(SSD_GROUPS)], axis=1) * ea_e
        states = [states[g] * ea_e[q - 1:q, g * group_w:(g + 1) * group_w] + d_states[g]
                  for g in range(SSD_GROUPS)]
        rows = pl.ds(k * q, q)
        gated = (y_local + y_off) * g_ref[0, rows, :].astype(F32)
        ms = jnp.mean(gated * gated, axis=-1, keepdims=True)
        y_ref[0, rows, :] = (gated * lax.rsqrt(ms + EPS) * nw_ref[...]).astype(y_ref.dtype)
    for g in range(SSD_GROUPS):
        state_ref[:, g * group_w:(g + 1) * group_w] = states[g]


def _ssd(u, dt_raw, gate, dt_bias, a_log, d_skip, ssd_norm_w, d_ssd):
    b, l, conv_dim = u.shape
    heads = d_ssd // HEAD_DIM
    rows = SSD_CHUNK * SSD_STEP_CHUNKS
    assert l % rows == 0
    pad = LANES - heads
    dtb = jnp.pad(dt_bias.astype(F32), (0, pad)).reshape(1, LANES)
    alog = jnp.pad(a_log.astype(F32), (0, pad)).reshape(1, LANES)
    dskip_e = jnp.repeat(d_skip.astype(F32), HEAD_DIM).reshape(1, d_ssd)
    lane_head = lax.broadcasted_iota(jnp.int32, (LANES, d_ssd), 1) // HEAD_DIM
    row_head = lax.broadcasted_iota(jnp.int32, (LANES, d_ssd), 0)
    expand = jnp.where(lane_head == row_head, 1.0, 0.0).astype(BF16)
    chunk = lambda bi, ci: (bi, ci, 0)
    fixed = lambda bi, ci: (0, 0)
    return pl.pallas_call(
        functools.partial(_ssd_kernel, d_ssd=d_ssd),
        grid=(b, l // rows),
        in_specs=[
            pl.BlockSpec((1, rows, conv_dim), chunk),
            pl.BlockSpec((1, rows, LANES), chunk),
            pl.BlockSpec((1, rows, d_ssd), chunk),
            pl.BlockSpec((1, LANES), fixed),
            pl.BlockSpec((1, LANES), fixed),
            pl.BlockSpec((1, d_ssd), fixed),
            pl.BlockSpec((1, d_ssd), fixed),
            pl.BlockSpec((LANES, d_ssd), fixed),
        ],
        out_specs=pl.BlockSpec((1, rows, d_ssd), chunk),
        out_shape=jax.ShapeDtypeStruct((b, l, d_ssd), BF16),
        scratch_shapes=[pltpu.VMEM((SSD_STATE, d_ssd), F32)],
        compiler_params=pltpu.CompilerParams(
            dimension_semantics=("arbitrary", "arbitrary"), vmem_limit_bytes=VMEM_LIMIT),
        name="ssd",
    )(u, dt_raw, gate, dtb, alog, dskip_e, ssd_norm_w.astype(F32).reshape(1, d_ssd), expand)


def _out_proj_kernel(o_ref, g_ref, y_ref, x_ref, nw_ref, w_ref, out_ref):
    d_sb = o_ref.shape[-1]
    gated = o_ref[...].astype(F32) * g_ref[...].astype(F32)
    ms = jnp.mean(gated * gated, axis=-1, keepdims=True)
    y_sb = (gated * lax.rsqrt(ms + EPS) * nw_ref[...]).astype(BF16)
    out_ref[...] = (x_ref[...]
                    + jnp.dot(y_sb, w_ref[:d_sb, :], preferred_element_type=F32)
                    + jnp.dot(y_ref[...], w_ref[d_sb:, :], preferred_element_type=F32))


def _out_proj(o_sb, gate_sb, y_ssd, x2, sb_norm_w, w_out):
    m, d = x2.shape
    d_sb = o_sb.shape[-1]
    d_ssd = y_ssd.shape[-1]
    row = lambda i: (i, 0)
    fixed = lambda i: (0, 0)
    return pl.pallas_call(
        _out_proj_kernel,
        grid=(m // ROW_TILE,),
        in_specs=[
            pl.BlockSpec((ROW_TILE, d_sb), row),
            pl.BlockSpec((ROW_TILE, d_sb), row),
            pl.BlockSpec((ROW_TILE, d_ssd), row),
            pl.BlockSpec((ROW_TILE, d), row),
            pl.BlockSpec((1, d_sb), fixed),
            pl.BlockSpec(w_out.shape, fixed, pipeline_mode=pl.Buffered(1)),
        ],
        out_specs=pl.BlockSpec((ROW_TILE, d), row),
        out_shape=jax.ShapeDtypeStruct((m, d), F32),
        compiler_params=pltpu.CompilerParams(
            dimension_semantics=("arbitrary",), vmem_limit_bytes=VMEM_LIMIT),
        name="out_proj",
    )(o_sb, gate_sb, y_ssd, x2, sb_norm_w.astype(F32).reshape(1, d_sb), w_out)


def _layer(x, norm_w, w_in, q_norm_w, k_norm_w, conv_w, conv_b, dt_bias, a_log, d_skip,
           sb_norm_w, ssd_norm_w, w_out):
    b, l, d = x.shape
    d_sb = sb_norm_w.shape[0]
    d_ssd = ssd_norm_w.shape[0]
    conv_dim = conv_w.shape[1]
    heads = dt_bias.shape[0]
    main = 4 * d_sb + d_ssd + conv_dim
    assert w_in.shape[1] == main + heads and heads <= LANES
    assert conv_dim == d_ssd + 2 * SSD_GROUPS * SSD_STATE and heads * HEAD_DIM == d_ssd
    m = b * l
    x2 = x.reshape(m, d)
    w_main = w_in[:, :main].astype(BF16)
    w_dt = jnp.pad(w_in[:, main:], ((0, 0), (0, LANES - heads))).astype(BF16)
    assert l % ROW_TILE == 0
    q, k, v, gate_sb, gate_ssd, u, dt_raw = _in_proj(
        x2, norm_w.astype(F32), w_main, w_dt, conv_w, conv_b, d_sb, d_ssd, l)
    o_sb = _sb_attention(q.reshape(b, l, d_sb), k.reshape(b, l, d_sb), v.reshape(b, l, d_sb),
                         q_norm_w, k_norm_w)
    y_ssd = _ssd(u.reshape(b, l, conv_dim), dt_raw.reshape(b, l, LANES),
                 gate_ssd.reshape(b, l, d_ssd), dt_bias, a_log, d_skip, ssd_norm_w, d_ssd)
    out = _out_proj(o_sb.reshape(m, d_sb), gate_sb, y_ssd.reshape(m, d_ssd), x2, sb_norm_w,
                    w_out.astype(BF16))
    return out.reshape(b, l, d)


def kernel(x, norm_w, w_in, q_norm_w, k_norm_w, conv_w, conv_b, dt_bias, A_log, D_skip,
           sb_norm_w, ssd_norm_w, w_out):
    for layer in range(norm_w.shape[0]):
        x = _layer(x, norm_w[layer], w_in[layer], q_norm_w[layer], k_norm_w[layer],
                   conv_w[layer], conv_b[layer], dt_bias[layer], A_log[layer], D_skip[layer],
                   sb_norm_w[layer], ssd_norm_w[layer], w_out[layer])
    return x
```

```python
import functools
import math

import jax
import jax.numpy as jnp
from jax import lax
from jax.experimental import pallas as pl
from jax.experimental.pallas import tpu as pltpu

F32 = jnp.float32
BF16 = jnp.bfloat16

EPS = 1e-6
LANES = 128
HEAD_DIM = 64
SSD_STATE = 128
SSD_GROUPS = 2
SSD_CHUNK = 128
SSD_STEP_CHUNKS = 4
CONV_W = 4
CONV_HALO = 8

ROW_TILE = 512
ATT_TILE = 128
ATT_WINDOW = 256
ATT_UNROLL = 8
KEY_NORM_ROWS = 2048
VMEM_LIMIT = 56 * 1024 * 1024
LOG2E = 1.4426950408889634
LOG2_STICK_FLOOR = -150.1
MASKED_LOG = -1e30


def _silu(z):
    return z * (1.0 / (1.0 + jnp.exp(-z)))


def _softplus(z):
    return jnp.maximum(z, 0.0) + jnp.log(1.0 + jnp.exp(-jnp.abs(z)))


def _in_proj_kernel(x_ref, nw_ref, w_ref, wdt_ref,
                    q_ref, k_ref, v_ref, gsb_ref, gssd_ref, xbc_ref, dt_ref):
    x = x_ref[...]
    ms = jnp.mean(x * x, axis=-1, keepdims=True)
    hn = (x * lax.rsqrt(ms + EPS) * nw_ref[...]).astype(BF16)
    start = 0
    for ref, gate in ((q_ref, False), (k_ref, False), (v_ref, False),
                      (gsb_ref, True), (gssd_ref, True), (xbc_ref, False)):
        width = ref.shape[-1]
        p = jnp.dot(hn, w_ref[:, start:start + width], preferred_element_type=F32)
        ref[...] = (_silu(p) if gate else p).astype(ref.dtype)
        start += width
    dt_ref[...] = jnp.dot(hn, wdt_ref[...], preferred_element_type=F32)


def _in_proj(x2, norm_w, w_main, w_dt, d_sb, d_ssd, conv_dim):
    m, d = x2.shape
    widths = (d_sb, d_sb, d_sb, d_sb, d_ssd, conv_dim)
    row = lambda i: (i, 0)
    fixed = lambda i: (0, 0)
    out_shape = [jax.ShapeDtypeStruct((m, w), BF16) for w in widths]
    out_shape.append(jax.ShapeDtypeStruct((m, LANES), F32))
    out_specs = [pl.BlockSpec((ROW_TILE, w), row) for w in widths]
    out_specs.append(pl.BlockSpec((ROW_TILE, LANES), row))
    return pl.pallas_call(
        _in_proj_kernel,
        grid=(m // ROW_TILE,),
        in_specs=[
            pl.BlockSpec((ROW_TILE, d), row),
            pl.BlockSpec((1, d), fixed),
            pl.BlockSpec(w_main.shape, fixed, pipeline_mode=pl.Buffered(1)),
            pl.BlockSpec(w_dt.shape, fixed, pipeline_mode=pl.Buffered(1)),
        ],
        out_specs=out_specs,
        out_shape=out_shape,
        compiler_params=pltpu.CompilerParams(
            dimension_semantics=("arbitrary",), vmem_limit_bytes=VMEM_LIMIT),
        name="in_proj",
    )(x2, norm_w.reshape(1, d), w_main, w_dt)


def _attn_kernel(q_ref, k_ref, v_ref, qnw_ref, knw_ref, u_ref, o_ref, kn_ref, *, seq_len):
    tq, win = ATT_TILE, ATT_WINDOW
    first = lax.broadcasted_iota(jnp.int32, (1, LANES), 1) < HEAD_DIM
    col_minus_row = (lax.broadcasted_iota(jnp.int32, (tq, win), 1)
                     - lax.broadcasted_iota(jnp.int32, (tq, win), 0))

    same_head = (lax.broadcasted_iota(jnp.int32, (LANES, LANES), 0) < HEAD_DIM) == first
    head_ones = jnp.where(same_head, 1.0, 0.0).astype(BF16)

    def head_norm(t, w):
        ss = jnp.dot((t * t).astype(BF16), head_ones, preferred_element_type=F32)
        return t * lax.rsqrt(ss * (1.0 / HEAD_DIM) + EPS) * w

    def norm_keys(i, carry):
        r = pl.multiple_of(i * KEY_NORM_ROWS, KEY_NORM_ROWS)
        kk = k_ref[0, pl.ds(r, KEY_NORM_ROWS), :].astype(F32)
        kn_ref[pl.ds(r, KEY_NORM_ROWS), :] = head_norm(kk, knw_ref[...]).astype(BF16)
        return carry

    lax.fori_loop(0, seq_len // KEY_NORM_ROWS, norm_keys, 0)

    def stage_scores(qm, start, width):
        kb = kn_ref[pl.ds(pl.multiple_of(start, tq), width), :]
        return lax.dot_general(qm, kb, (((1,), (1,)), ((), ())), preferred_element_type=F32)

    def stage_log_terms(z, mask):
        sp = jnp.maximum(z, 0.0) + jnp.log(1.0 + jnp.exp2(-jnp.abs(z))) * LOG2E
        lq = z - sp
        if mask is not None:
            sp = jnp.where(mask, sp, 0.0)
        return sp.astype(BF16), lq, jnp.sum(sp, axis=1, keepdims=True)

    def stage_cumsum(sp_bf):
        width = sp_bf.shape[1]
        return jnp.dot(sp_bf, u_ref[:width, :width], preferred_element_type=F32)

    def stage_weights(lq, rl, mask):
        w = jnp.exp2(lq + rl)
        if mask is not None:
            w = jnp.where(mask, w, 0.0)
        return w.astype(BF16)

    def stage_values(w_bf, start):
        vb = v_ref[0, pl.ds(pl.multiple_of(start, tq), w_bf.shape[1]), :]
        return jnp.dot(w_bf, vb, preferred_element_type=F32)

    def key_window(qm, start, width):
        sp_bf, lq, s = stage_log_terms(stage_scores(qm, start, width), None)
        w_bf = stage_weights(lq, stage_cumsum(sp_bf), None)
        return stage_values(w_bf, start), s

    def log_stick_left(cs):
        return jnp.max(jnp.maximum(cs[0], cs[1]))

    def query_tiles(it, carry):
        pairs = []
        for b in range(ATT_UNROLL):
            t = it * ATT_UNROLL + b
            row0 = t * tq
            qn = head_norm(q_ref[0, pl.ds(pl.multiple_of(row0, tq), tq), :].astype(F32),
                           qnw_ref[...])
            qn = qn * (LOG2E / math.sqrt(HEAD_DIM))
            start1 = jnp.maximum(t - 1, 0) * tq
            start2 = jnp.maximum(t - 2, 0) * tq
            mask = col_minus_row < row0 - start1
            for h in range(2):
                qm = jnp.where(first if h == 0 else jnp.logical_not(first), qn, 0.0).astype(BF16)
                pairs.append({"t": t, "qm": qm,
                              "wins": ((start1, win, mask), (start2, tq, None))})
        for step in range(len(pairs) + 4):
            for lag, p in enumerate(pairs):
                stage = step - lag
                if stage == 0:
                    p["z"] = [stage_scores(p["qm"], s, n) for s, n, _ in p["wins"]]
                elif stage == 1:
                    p["log"] = [stage_log_terms(z, w[2]) for z, w in zip(p["z"], p["wins"])]
                elif stage == 2:
                    p["rl"] = [stage_cumsum(t[0]) for t in p["log"]]
                elif stage == 3:
                    p["w"] = [stage_weights(t[1], rl, w[2])
                              for t, rl, w in zip(p["log"], p["rl"], p["wins"])]
                elif stage == 4:
                    p["pv"] = [stage_values(w, win_[0]) for w, win_ in zip(p["w"], p["wins"])]

        starts = []
        for b in range(ATT_UNROLL):
            t = pairs[2 * b]["t"]
            has_second = t >= 2
            accs, cs = [], []
            for p in pairs[2 * b:2 * b + 2]:
                c = -p["log"][0][2]
                accs.append(p["pv"][0] + jnp.where(has_second, jnp.exp2(c) * p["pv"][1], 0.0))
                cs.append(c - jnp.where(has_second, p["log"][1][2], 0.0))
            starts.append((t - 3, log_stick_left(cs), accs[0], accs[1], cs[0], cs[1]))

        def cond(state):
            return jnp.logical_and(state[0] >= 0, state[1] > LOG2_STICK_FLOOR)

        for b in range(ATT_UNROLL):
            qms = [pairs[2 * b]["qm"], pairs[2 * b + 1]["qm"]]

            def body(state, qms=qms):
                j, _, a0, a1, c0, c1 = state
                accs, cs = [], []
                for h, (a, c) in enumerate(((a0, c0), (a1, c1))):
                    pv, s = key_window(qms[h], j * tq, tq)
                    accs.append(a + jnp.exp2(c) * pv)
                    cs.append(c - s)
                return (j - 1, log_stick_left(cs), accs[0], accs[1], cs[0], cs[1])

            state = lax.while_loop(cond, body, starts[b])
            row0 = pl.multiple_of(pairs[2 * b]["t"] * tq, tq)
            o_ref[0, pl.ds(row0, tq), :] = (
                jnp.where(first, state[2], state[3]).astype(o_ref.dtype))
        return carry

    lax.fori_loop(0, seq_len // (tq * ATT_UNROLL), query_tiles, 0)


def _sb_attention(q, k, v, q_norm_w, k_norm_w):
    b, l, d = q.shape
    assert l % KEY_NORM_ROWS == 0 and l % (ATT_TILE * ATT_UNROLL) == 0 and d % LANES == 0
    blk = ATT_WINDOW
    r_id = lax.broadcasted_iota(jnp.int32, (blk, blk), 0)
    c_id = lax.broadcasted_iota(jnp.int32, (blk, blk), 1)
    u_neg = jnp.where(r_id > c_id, -1.0, 0.0).astype(BF16)
    qnw = jnp.tile(q_norm_w.astype(F32), LANES // HEAD_DIM).reshape(1, LANES)
    knw = jnp.tile(k_norm_w.astype(F32), LANES // HEAD_DIM).reshape(1, LANES)
    pair = lambda bi, p: (bi, 0, p)
    fixed = lambda bi, p: (0, 0)
    return pl.pallas_call(
        functools.partial(_attn_kernel, seq_len=l),
        grid=(b, d // LANES),
        in_specs=[
            pl.BlockSpec((1, l, LANES), pair),
            pl.BlockSpec((1, l, LANES), pair),
            pl.BlockSpec((1, l, LANES), pair),
            pl.BlockSpec((1, LANES), fixed),
            pl.BlockSpec((1, LANES), fixed),
            pl.BlockSpec((blk, blk), fixed),
        ],
        out_specs=pl.BlockSpec((1, l, LANES), pair),
        out_shape=jax.ShapeDtypeStruct((b, l, d), BF16),
        scratch_shapes=[pltpu.VMEM((l, LANES), BF16)],
        compiler_params=pltpu.CompilerParams(
            dimension_semantics=("arbitrary", "arbitrary"), vmem_limit_bytes=VMEM_LIMIT),
        name="sb_attn",
    )(q, k, v, qnw, knw, u_neg)


def _ssd_kernel(xbc_ref, dt_ref, g_ref, cw_ref, cb_ref, dtb_ref, alog_ref, dskip_ref, nw_ref,
                expand_ref, y_ref, ext_ref, state_ref, *, d_ssd):
    ci = pl.program_id(1)
    q = SSD_CHUNK
    gw = SSD_STATE
    heads_per_group = d_ssd // HEAD_DIM // SSD_GROUPS
    group_w = heads_per_group * HEAD_DIM
    step_rows = xbc_ref.shape[1]

    @pl.when(ci == 0)
    def _():
        ext_ref[0:CONV_HALO, :] = jnp.zeros((CONV_HALO, ext_ref.shape[1]), F32)
        state_ref[...] = jnp.zeros(state_ref.shape, F32)

    @pl.when(ci > 0)
    def _():
        ext_ref[0:CONV_HALO, :] = ext_ref[step_rows:step_rows + CONV_HALO, :]

    ext_ref[CONV_HALO:, :] = xbc_ref[0].astype(F32)

    def conv_silu(k):
        ext = ext_ref[pl.ds(k * q, q + CONV_HALO), :]
        conv = cb_ref[...] + ext[CONV_HALO:, :] * cw_ref[CONV_W - 1:CONV_W, :]
        for back in range(1, CONV_W):
            tap = CONV_W - 1 - back
            conv = conv + pltpu.roll(ext, back, axis=0)[CONV_HALO:, :] * cw_ref[tap:tap + 1, :]
        return _silu(conv)

    r_id = lax.broadcasted_iota(jnp.int32, (q, q), 0)
    c_id = lax.broadcasted_iota(jnp.int32, (q, q), 1)
    lower = c_id <= r_id
    tri = jnp.where(lower, 1.0, 0.0).astype(BF16)
    first = lax.broadcasted_iota(jnp.int32, (1, LANES), 1) < HEAD_DIM
    neg_a = -jnp.exp(alog_ref[...])

    def expand(t):
        return jnp.dot(t.astype(BF16), expand_ref[...], preferred_element_type=F32)

    def chunk_local(k, u):
        rows = pl.ds(k * q, q)
        xs = u[:, :d_ssd]
        b_all = u[:, d_ssd:d_ssd + SSD_GROUPS * gw].astype(BF16)
        c_all = u[:, d_ssd + SSD_GROUPS * gw:].astype(BF16)

        dt = _softplus(dt_ref[0, rows, :] + dtb_ref[...])
        d_a = dt * neg_a
        d_a_hi = d_a.astype(BF16)
        d_a_lo = (d_a - d_a_hi.astype(F32)).astype(BF16)
        a_cs = (jnp.dot(tri, d_a_hi, preferred_element_type=F32)
                + jnp.dot(tri, d_a_lo, preferred_element_type=F32))
        a_cs_t = a_cs.T
        a_last = a_cs[q - 1:q, :]
        dt_e = expand(dt)
        fac_e = expand(dt * jnp.exp(a_last - a_cs))
        ea_e = expand(jnp.exp(a_cs))
        xdt = xs * dt_e
        xfac = (xs * fac_e).astype(BF16)

        y_parts, cgs, d_states = [], [], []
        for g in range(SSD_GROUPS):
            bg = b_all[:, g * gw:(g + 1) * gw]
            cg = c_all[:, g * gw:(g + 1) * gw]
            cb = lax.dot_general(cg, bg, (((1,), (1,)), ((), ())),
                                 preferred_element_type=F32)
            bg_t = u[:, d_ssd + g * gw:d_ssd + (g + 1) * gw].T.astype(BF16)
            lo = g * group_w
            cgs.append(cg)
            d_states.append(jnp.dot(bg_t, xfac[:, lo:lo + group_w],
                                    preferred_element_type=F32))
            for m in range(heads_per_group // 2):
                base = lo + m * LANES
                x_pair = xdt[:, base:base + LANES]
                y_pair = None
                for h in range(2):
                    kh = base // HEAD_DIM + h
                    seg = a_cs[:, kh:kh + 1] - a_cs_t[kh:kh + 1, :]
                    decay = jnp.exp(jnp.where(lower, seg, MASKED_LOG))
                    wts = (cb * decay).astype(BF16)
                    x_h = jnp.where(first if h == 0 else jnp.logical_not(first), x_pair, 0.0)
                    part = jnp.dot(wts, x_h.astype(BF16), preferred_element_type=F32)
                    y_pair = part if y_pair is None else y_pair + part
                y_parts.append(y_pair)
        y_local = jnp.concatenate(y_parts, axis=1) + dskip_ref[...] * xs
        return y_local, cgs, d_states, ea_e

    local = []
    u_next = conv_silu(0)
    for k in range(SSD_STEP_CHUNKS):
        u = u_next
        if k + 1 < SSD_STEP_CHUNKS:
            u_next = conv_silu(k + 1)
        local.append(chunk_local(k, u))

    states = [state_ref[:, g * group_w:(g + 1) * group_w] for g in range(SSD_GROUPS)]
    for k, (y_local, cgs, d_states, ea_e) in enumerate(local):
        y_off = jnp.concatenate(
            [jnp.dot(cgs[g], states[g].astype(BF16), preferred_element_type=F32)
             for g in range(SSD_GROUPS)], axis=1) * ea_e
        states = [states[g] * ea_e[q - 1:q, g * group_w:(g + 1) * group_w] + d_states[g]
                  for g in range(SSD_GROUPS)]
        rows = pl.ds(k * q, q)
        gated = (y_local + y_off) * g_ref[0, rows, :].astype(F32)
        ms = jnp.mean(gated * gated, axis=-1, keepdims=True)
        y_ref[0, rows, :] = (gated * lax.rsqrt(ms + EPS) * nw_ref[...]).astype(y_ref.dtype)
    for g in range(SSD_GROUPS):
        state_ref[:, g * group_w:(g + 1) * group_w] = states[g]


def _ssd(xbc, dt_raw, gate, conv_w, conv_b, dt_bias, a_log, d_skip, ssd_norm_w, d_ssd):
    b, l, conv_dim = xbc.shape
    heads = d_ssd // HEAD_DIM
    rows = SSD_CHUNK * SSD_STEP_CHUNKS
    assert l % rows == 0
    pad = LANES - heads
    dtb = jnp.pad(dt_bias.astype(F32), (0, pad)).reshape(1, LANES)
    alog = jnp.pad(a_log.astype(F32), (0, pad)).reshape(1, LANES)
    dskip_e = jnp.repeat(d_skip.astype(F32), HEAD_DIM).reshape(1, d_ssd)
    lane_head = lax.broadcasted_iota(jnp.int32, (LANES, d_ssd), 1) // HEAD_DIM
    row_head = lax.broadcasted_iota(jnp.int32, (LANES, d_ssd), 0)
    expand = jnp.where(lane_head == row_head, 1.0, 0.0).astype(BF16)
    chunk = lambda bi, ci: (bi, ci, 0)
    fixed = lambda bi, ci: (0, 0)
    return pl.pallas_call(
        functools.partial(_ssd_kernel, d_ssd=d_ssd),
        grid=(b, l // rows),
        in_specs=[
            pl.BlockSpec((1, rows, conv_dim), chunk),
            pl.BlockSpec((1, rows, LANES), chunk),
            pl.BlockSpec((1, rows, d_ssd), chunk),
            pl.BlockSpec((CONV_W, conv_dim), fixed),
            pl.BlockSpec((1, conv_dim), fixed),
            pl.BlockSpec((1, LANES), fixed),
            pl.BlockSpec((1, LANES), fixed),
            pl.BlockSpec((1, d_ssd), fixed),
            pl.BlockSpec((1, d_ssd), fixed),
            pl.BlockSpec((LANES, d_ssd), fixed),
        ],
        out_specs=pl.BlockSpec((1, rows, d_ssd), chunk),
        out_shape=jax.ShapeDtypeStruct((b, l, d_ssd), BF16),
        scratch_shapes=[pltpu.VMEM((rows + CONV_HALO, conv_dim), F32),
                        pltpu.VMEM((SSD_STATE, d_ssd), F32)],
        compiler_params=pltpu.CompilerParams(
            dimension_semantics=("arbitrary", "arbitrary"), vmem_limit_bytes=VMEM_LIMIT),
        name="ssd",
    )(xbc, dt_raw, gate, conv_w.astype(F32), conv_b.astype(F32).reshape(1, conv_dim),
      dtb, alog, dskip_e, ssd_norm_w.astype(F32).reshape(1, d_ssd), expand)


def _out_proj_kernel(o_ref, g_ref, y_ref, x_ref, nw_ref, w_ref, out_ref):
    d_sb = o_ref.shape[-1]
    gated = o_ref[...].astype(F32) * g_ref[...].astype(F32)
    ms = jnp.mean(gated * gated, axis=-1, keepdims=True)
    y_sb = (gated * lax.rsqrt(ms + EPS) * nw_ref[...]).astype(BF16)
    out_ref[...] = (x_ref[...]
                    + jnp.dot(y_sb, w_ref[:d_sb, :], preferred_element_type=F32)
                    + jnp.dot(y_ref[...], w_ref[d_sb:, :], preferred_element_type=F32))


def _out_proj(o_sb, gate_sb, y_ssd, x2, sb_norm_w, w_out):
    m, d = x2.shape
    d_sb = o_sb.shape[-1]
    d_ssd = y_ssd.shape[-1]
    row = lambda i: (i, 0)
    fixed = lambda i: (0, 0)
    return pl.pallas_call(
        _out_proj_kernel,
        grid=(m // ROW_TILE,),
        in_specs=[
            pl.BlockSpec((ROW_TILE, d_sb), row),
            pl.BlockSpec((ROW_TILE, d_sb), row),
            pl.BlockSpec((ROW_TILE, d_ssd), row),
            pl.BlockSpec((ROW_TILE, d), row),
            pl.BlockSpec((1, d_sb), fixed),
            pl.BlockSpec(w_out.shape, fixed, pipeline_mode=pl.Buffered(1)),
        ],
        out_specs=pl.BlockSpec((ROW_TILE, d), row),
        out_shape=jax.ShapeDtypeStruct((m, d), F32),
        compiler_params=pltpu.CompilerParams(
            dimension_semantics=("arbitrary",), vmem_limit_bytes=VMEM_LIMIT),
        name="out_proj",
    )(o_sb, gate_sb, y_ssd, x2, sb_norm_w.astype(F32).reshape(1, d_sb), w_out)


def _layer(x, norm_w, w_in, q_norm_w, k_norm_w, conv_w, conv_b, dt_bias, a_log, d_skip,
           sb_norm_w, ssd_norm_w, w_out):
    b, l, d = x.shape
    d_sb = sb_norm_w.shape[0]
    d_ssd = ssd_norm_w.shape[0]
    conv_dim = conv_w.shape[1]
    heads = dt_bias.shape[0]
    main = 4 * d_sb + d_ssd + conv_dim
    assert w_in.shape[1] == main + heads and heads <= LANES
    assert conv_dim == d_ssd + 2 * SSD_GROUPS * SSD_STATE and heads * HEAD_DIM == d_ssd
    m = b * l
    x2 = x.reshape(m, d)
    w_main = w_in[:, :main].astype(BF16)
    w_dt = jnp.pad(w_in[:, main:], ((0, 0), (0, LANES - heads))).astype(BF16)
    q, k, v, gate_sb, gate_ssd, xbc, dt_raw = _in_proj(
        x2, norm_w.astype(F32), w_main, w_dt, d_sb, d_ssd, conv_dim)
    o_sb = _sb_attention(q.reshape(b, l, d_sb), k.reshape(b, l, d_sb), v.reshape(b, l, d_sb),
                         q_norm_w, k_norm_w)
    y_ssd = _ssd(xbc.reshape(b, l, conv_dim), dt_raw.reshape(b, l, LANES),
                 gate_ssd.reshape(b, l, d_ssd), conv_w, conv_b, dt_bias, a_log, d_skip,
                 ssd_norm_w, d_ssd)
    out = _out_proj(o_sb.reshape(m, d_sb), gate_sb, y_ssd.reshape(m, d_ssd), x2, sb_norm_w,
                    w_out.astype(BF16))
    return out.reshape(b, l, d)


def kernel(x, norm_w, w_in, q_norm_w, k_norm_w, conv_w, conv_b, dt_bias, A_log, D_skip,
           sb_norm_w, ssd_norm_w, w_out):
    for layer in range(norm_w.shape[0]):
        x = _layer(x, norm_w[layer], w_in[layer], q_norm_w[layer], k_norm_w[layer],
                   conv_w[layer], conv_b[layer], dt_bias[layer], A_log[layer], D_skip[layer],
                   sb_norm_w[layer], ssd_norm_w[layer], w_out[layer])
    return x
```

```python
import functools
import math

import jax
import jax.numpy as jnp
from jax import lax
from jax.experimental import pallas as pl
from jax.experimental.pallas import tpu as pltpu

F32 = jnp.float32
BF16 = jnp.bfloat16

EPS = 1e-6
LANES = 128
HEAD_DIM = 64
SSD_STATE = 128
SSD_GROUPS = 2
SSD_CHUNK = 128
SSD_STEP_CHUNKS = 4
CONV_W = 4
CONV_HALO = 8

ROW_TILE = 512
ATT_TILE = 128
ATT_WINDOW = 256
ATT_UNROLL = 8
KEY_NORM_ROWS = 2048
VMEM_LIMIT = 56 * 1024 * 1024
LOG2E = 1.4426950408889634
LOG2_STICK_FLOOR = -150.1
MASKED_LOG = -1e30


def _silu(z):
    return z * (1.0 / (1.0 + jnp.exp(-z)))


def _softplus(z):
    return jnp.maximum(z, 0.0) + jnp.log(1.0 + jnp.exp(-jnp.abs(z)))


def _in_proj_kernel(x_ref, nw_ref, w_ref, wdt_ref,
                    q_ref, k_ref, v_ref, gsb_ref, gssd_ref, xbc_ref, dt_ref):
    x = x_ref[...]
    ms = jnp.mean(x * x, axis=-1, keepdims=True)
    hn = (x * lax.rsqrt(ms + EPS) * nw_ref[...]).astype(BF16)
    start = 0
    for ref, gate in ((q_ref, False), (k_ref, False), (v_ref, False),
                      (gsb_ref, True), (gssd_ref, True), (xbc_ref, False)):
        width = ref.shape[-1]
        p = jnp.dot(hn, w_ref[:, start:start + width], preferred_element_type=F32)
        ref[...] = (_silu(p) if gate else p).astype(ref.dtype)
        start += width
    dt_ref[...] = jnp.dot(hn, wdt_ref[...], preferred_element_type=F32)


def _in_proj(x2, norm_w, w_main, w_dt, d_sb, d_ssd, conv_dim):
    m, d = x2.shape
    widths = (d_sb, d_sb, d_sb, d_sb, d_ssd, conv_dim)
    row = lambda i: (i, 0)
    fixed = lambda i: (0, 0)
    out_shape = [jax.ShapeDtypeStruct((m, w), BF16) for w in widths]
    out_shape.append(jax.ShapeDtypeStruct((m, LANES), F32))
    out_specs = [pl.BlockSpec((ROW_TILE, w), row) for w in widths]
    out_specs.append(pl.BlockSpec((ROW_TILE, LANES), row))
    return pl.pallas_call(
        _in_proj_kernel,
        grid=(m // ROW_TILE,),
        in_specs=[
            pl.BlockSpec((ROW_TILE, d), row),
            pl.BlockSpec((1, d), fixed),
            pl.BlockSpec(w_main.shape, fixed, pipeline_mode=pl.Buffered(1)),
            pl.BlockSpec(w_dt.shape, fixed, pipeline_mode=pl.Buffered(1)),
        ],
        out_specs=out_specs,
        out_shape=out_shape,
        compiler_params=pltpu.CompilerParams(
            dimension_semantics=("arbitrary",), vmem_limit_bytes=VMEM_LIMIT),
        name="in_proj",
    )(x2, norm_w.reshape(1, d), w_main, w_dt)


def _attn_kernel(q_ref, k_ref, v_ref, qnw_ref, knw_ref, u_ref, o_ref, kn_ref, *, seq_len):
    tq, win = ATT_TILE, ATT_WINDOW
    first = lax.broadcasted_iota(jnp.int32, (1, LANES), 1) < HEAD_DIM
    col_minus_row = (lax.broadcasted_iota(jnp.int32, (tq, win), 1)
                     - lax.broadcasted_iota(jnp.int32, (tq, win), 0))

    same_head = (lax.broadcasted_iota(jnp.int32, (LANES, LANES), 0) < HEAD_DIM) == first
    head_ones = jnp.where(same_head, 1.0, 0.0).astype(BF16)

    def head_norm(t, w):
        ss = jnp.dot((t * t).astype(BF16), head_ones, preferred_element_type=F32)
        return t * lax.rsqrt(ss * (1.0 / HEAD_DIM) + EPS) * w

    def norm_keys(i, carry):
        r = pl.multiple_of(i * KEY_NORM_ROWS, KEY_NORM_ROWS)
        kk = k_ref[0, pl.ds(r, KEY_NORM_ROWS), :].astype(F32)
        kn_ref[pl.ds(r, KEY_NORM_ROWS), :] = head_norm(kk, knw_ref[...]).astype(BF16)
        return carry

    lax.fori_loop(0, seq_len // KEY_NORM_ROWS, norm_keys, 0)

    def stage_scores(qm, start, width):
        kb = kn_ref[pl.ds(pl.multiple_of(start, tq), width), :]
        return lax.dot_general(qm, kb, (((1,), (1,)), ((), ())), preferred_element_type=F32)

    def stage_log_terms(z, mask):
        sp = jnp.maximum(z, 0.0) + jnp.log(1.0 + jnp.exp2(-jnp.abs(z))) * LOG2E
        lq = z - sp
        if mask is not None:
            sp = jnp.where(mask, sp, 0.0)
        return sp.astype(BF16), lq, jnp.sum(sp, axis=1, keepdims=True)

    def stage_cumsum(sp_bf):
        width = sp_bf.shape[1]
        return jnp.dot(sp_bf, u_ref[:width, :width], preferred_element_type=F32)

    def stage_weights(lq, rl, mask):
        w = jnp.exp2(lq + rl)
        if mask is not None:
            w = jnp.where(mask, w, 0.0)
        return w.astype(BF16)

    def stage_values(w_bf, start):
        vb = v_ref[0, pl.ds(pl.multiple_of(start, tq), w_bf.shape[1]), :]
        return jnp.dot(w_bf, vb, preferred_element_type=F32)

    def key_window(qm, start, width):
        sp_bf, lq, s = stage_log_terms(stage_scores(qm, start, width), None)
        w_bf = stage_weights(lq, stage_cumsum(sp_bf), None)
        return stage_values(w_bf, start), s

    def log_stick_left(cs):
        return jnp.max(jnp.maximum(cs[0], cs[1]))

    def query_tiles(it, carry):
        pairs = []
        for b in range(ATT_UNROLL):
            t = it * ATT_UNROLL + b
            row0 = t * tq
            qn = head_norm(q_ref[0, pl.ds(pl.multiple_of(row0, tq), tq), :].astype(F32),
                           qnw_ref[...])
            qn = qn * (LOG2E / math.sqrt(HEAD_DIM))
            start1 = (jnp.maximum(t - 1, 0) if b < 1 else t - 1) * tq
            start2 = (jnp.maximum(t - 2, 0) if b < 2 else t - 2) * tq
            mask = col_minus_row < (row0 - start1 if b < 1 else tq)
            for h in range(2):
                qm = jnp.where(first if h == 0 else jnp.logical_not(first), qn, 0.0).astype(BF16)
                pairs.append({"t": t, "qm": qm,
                              "wins": ((start1, win, mask), (start2, tq, None))})
        for step in range(len(pairs) + 4):
            for lag, p in enumerate(pairs):
                stage = step - lag
                if stage == 0:
                    p["z"] = [stage_scores(p["qm"], s, n) for s, n, _ in p["wins"]]
                elif stage == 1:
                    p["log"] = [stage_log_terms(z, w[2]) for z, w in zip(p["z"], p["wins"])]
                elif stage == 2:
                    p["rl"] = [stage_cumsum(t[0]) for t in p["log"]]
                elif stage == 3:
                    p["w"] = [stage_weights(t[1], rl, w[2])
                              for t, rl, w in zip(p["log"], p["rl"], p["wins"])]
                elif stage == 4:
                    p["pv"] = [stage_values(w, win_[0]) for w, win_ in zip(p["w"], p["wins"])]

        def store(t, acc0, acc1):
            row0 = pl.multiple_of(t * tq, tq)
            o_ref[0, pl.ds(row0, tq), :] = jnp.where(first, acc0, acc1).astype(o_ref.dtype)

        tiles = []
        stick = None
        for b in range(ATT_UNROLL):
            t = pairs[2 * b]["t"]
            accs, cs = [], []
            for p in pairs[2 * b:2 * b + 2]:
                c = -p["log"][0][2]
                second = jnp.exp2(c) * p["pv"][1]
                s_second = p["log"][1][2]
                if b < 2:
                    second = jnp.where(t >= 2, second, 0.0)
                    s_second = jnp.where(t >= 2, s_second, 0.0)
                accs.append(p["pv"][0] + second)
                cs.append(c - s_second)
            store(t, accs[0], accs[1])
            tiles.append((t, accs, cs))
            left = jnp.maximum(cs[0], cs[1])
            if b < 3:
                left = jnp.where(t >= 3, left, MASKED_LOG)
            stick = left if stick is None else jnp.maximum(stick, left)

        def cond(state):
            return jnp.logical_and(state[0] >= 0, state[1] > LOG2_STICK_FLOOR)

        @pl.when(jnp.max(stick) > LOG2_STICK_FLOOR)
        def _():
            for b, (t, accs, cs) in enumerate(tiles):
                qms = [pairs[2 * b]["qm"], pairs[2 * b + 1]["qm"]]

                def body(state, qms=qms):
                    j, _, a0, a1, c0, c1 = state
                    accs, cs = [], []
                    for h, (a, c) in enumerate(((a0, c0), (a1, c1))):
                        pv, s = key_window(qms[h], j * tq, tq)
                        accs.append(a + jnp.exp2(c) * pv)
                        cs.append(c - s)
                    return (j - 1, log_stick_left(cs), accs[0], accs[1], cs[0], cs[1])

                state = lax.while_loop(
                    cond, body, (t - 3, log_stick_left(cs), accs[0], accs[1], cs[0], cs[1]))
                store(t, state[2], state[3])
        return carry

    lax.fori_loop(0, seq_len // (tq * ATT_UNROLL), query_tiles, 0)


def _sb_attention(q, k, v, q_norm_w, k_norm_w):
    b, l, d = q.shape
    assert l % KEY_NORM_ROWS == 0 and l % (ATT_TILE * ATT_UNROLL) == 0 and d % LANES == 0
    blk = ATT_WINDOW
    r_id = lax.broadcasted_iota(jnp.int32, (blk, blk), 0)
    c_id = lax.broadcasted_iota(jnp.int32, (blk, blk), 1)
    u_neg = jnp.where(r_id > c_id, -1.0, 0.0).astype(BF16)
    qnw = jnp.tile(q_norm_w.astype(F32), LANES // HEAD_DIM).reshape(1, LANES)
    knw = jnp.tile(k_norm_w.astype(F32), LANES // HEAD_DIM).reshape(1, LANES)
    pair = lambda bi, p: (bi, 0, p)
    fixed = lambda bi, p: (0, 0)
    return pl.pallas_call(
        functools.partial(_attn_kernel, seq_len=l),
        grid=(b, d // LANES),
        in_specs=[
            pl.BlockSpec((1, l, LANES), pair),
            pl.BlockSpec((1, l, LANES), pair),
            pl.BlockSpec((1, l, LANES), pair),
            pl.BlockSpec((1, LANES), fixed),
            pl.BlockSpec((1, LANES), fixed),
            pl.BlockSpec((blk, blk), fixed),
        ],
        out_specs=pl.BlockSpec((1, l, LANES), pair),
        out_shape=jax.ShapeDtypeStruct((b, l, d), BF16),
        scratch_shapes=[pltpu.VMEM((l, LANES), BF16)],
        compiler_params=pltpu.CompilerParams(
            dimension_semantics=("arbitrary", "arbitrary"), vmem_limit_bytes=VMEM_LIMIT),
        name="sb_attn",
    )(q, k, v, qnw, knw, u_neg)


def _ssd_kernel(xbc_ref, dt_ref, g_ref, cw_ref, cb_ref, dtb_ref, alog_ref, dskip_ref, nw_ref,
                expand_ref, y_ref, ext_ref, state_ref, *, d_ssd):
    ci = pl.program_id(1)
    q = SSD_CHUNK
    gw = SSD_STATE
    heads_per_group = d_ssd // HEAD_DIM // SSD_GROUPS
    group_w = heads_per_group * HEAD_DIM
    step_rows = xbc_ref.shape[1]

    @pl.when(ci == 0)
    def _():
        ext_ref[0:CONV_HALO, :] = jnp.zeros((CONV_HALO, ext_ref.shape[1]), F32)
        state_ref[...] = jnp.zeros(state_ref.shape, F32)

    @pl.when(ci > 0)
    def _():
        ext_ref[0:CONV_HALO, :] = ext_ref[step_rows:step_rows + CONV_HALO, :]

    ext_ref[CONV_HALO:, :] = xbc_ref[0].astype(F32)

    def conv_silu(k):
        ext = ext_ref[pl.ds(k * q, q + CONV_HALO), :]
        conv = cb_ref[...] + ext[CONV_HALO:, :] * cw_ref[CONV_W - 1:CONV_W, :]
        for back in range(1, CONV_W):
            tap = CONV_W - 1 - back
            conv = conv + pltpu.roll(ext, back, axis=0)[CONV_HALO:, :] * cw_ref[tap:tap + 1, :]
        return _silu(conv)

    r_id = lax.broadcasted_iota(jnp.int32, (q, q), 0)
    c_id = lax.broadcasted_iota(jnp.int32, (q, q), 1)
    lower = c_id <= r_id
    tri = jnp.where(lower, 1.0, 0.0).astype(BF16)
    first = lax.broadcasted_iota(jnp.int32, (1, LANES), 1) < HEAD_DIM
    neg_a = -jnp.exp(alog_ref[...])

    def expand(t):
        return jnp.dot(t.astype(BF16), expand_ref[...], preferred_element_type=F32)

    def chunk_local(k, u):
        rows = pl.ds(k * q, q)
        xs = u[:, :d_ssd]
        b_all = u[:, d_ssd:d_ssd + SSD_GROUPS * gw].astype(BF16)
        c_all = u[:, d_ssd + SSD_GROUPS * gw:].astype(BF16)

        dt = _softplus(dt_ref[0, rows, :] + dtb_ref[...])
        d_a = dt * neg_a
        d_a_hi = d_a.astype(BF16)
        d_a_lo = (d_a - d_a_hi.astype(F32)).astype(BF16)
        a_cs = (jnp.dot(tri, d_a_hi, preferred_element_type=F32)
                + jnp.dot(tri, d_a_lo, preferred_element_type=F32))
        a_cs_t = a_cs.T
        a_last = a_cs[q - 1:q, :]
        dt_e = expand(dt)
        fac_e = expand(dt * jnp.exp(a_last - a_cs))
        ea_e = expand(jnp.exp(a_cs))
        xdt = xs * dt_e
        xfac = (xs * fac_e).astype(BF16)

        y_parts, cgs, d_states = [], [], []
        for g in range(SSD_GROUPS):
            bg = b_all[:, g * gw:(g + 1) * gw]
            cg = c_all[:, g * gw:(g + 1) * gw]
            cb = lax.dot_general(cg, bg, (((1,), (1,)), ((), ())),
                                 preferred_element_type=F32)
            bg_t = u[:, d_ssd + g * gw:d_ssd + (g + 1) * gw].T.astype(BF16)
            lo = g * group_w
            cgs.append(cg)
            d_states.append(jnp.dot(bg_t, xfac[:, lo:lo + group_w],
                                    preferred_element_type=F32))
            for m in range(heads_per_group // 2):
                base = lo + m * LANES
                x_pair = xdt[:, base:base + LANES]
                y_pair = None
                for h in range(2):
                    kh = base // HEAD_DIM + h
                    seg = a_cs[:, kh:kh + 1] - a_cs_t[kh:kh + 1, :]
                    decay = jnp.exp(jnp.where(lower, seg, MASKED_LOG))
                    wts = (cb * decay).astype(BF16)
                    x_h = jnp.where(first if h == 0 else jnp.logical_not(first), x_pair, 0.0)
                    part = jnp.dot(wts, x_h.astype(BF16), preferred_element_type=F32)
                    y_pair = part if y_pair is None else y_pair + part
                y_parts.append(y_pair)
        y_local = jnp.concatenate(y_parts, axis=1) + dskip_ref[...] * xs
        return y_local, cgs, d_states, ea_e

    local = []
    u_next = conv_silu(0)
    for k in range(SSD_STEP_CHUNKS):
        u = u_next
        if k + 1 < SSD_STEP_CHUNKS:
            u_next = conv_silu(k + 1)
        local.append(chunk_local(k, u))

    states = [state_ref[:, g * group_w:(g + 1) * group_w] for g in range(SSD_GROUPS)]
    for k, (y_local, cgs, d_states, ea_e) in enumerate(local):
        y_off = jnp.concatenate(
            [jnp.dot(cgs[g], states[g].astype(BF16), preferred_element_type=F32)
             for g in range(SSD_GROUPS)], axis=1) * ea_e
        states = [states[g] * ea_e[q - 1:q, g * group_w:(g + 1) * group_w] + d_states[g]
                  for g in range(SSD_GROUPS)]
        rows = pl.ds(k * q, q)
        gated = (y_local + y_off) * g_ref[0, rows, :].astype(F32)
        ms = jnp.mean(gated * gated, axis=-1, keepdims=True)
        y_ref[0, rows, :] = (gated * lax.rsqrt(ms + EPS) * nw_ref[...]).astype(y_ref.dtype)
    for g in range(SSD_GROUPS):
        state_ref[:, g * group_w:(g + 1) * group_w] = states[g]


def _ssd(xbc, dt_raw, gate, conv_w, conv_b, dt_bias, a_log, d_skip, ssd_norm_w, d_ssd):
    b, l, conv_dim = xbc.shape
    heads = d_ssd // HEAD_DIM
    rows = SSD_CHUNK * SSD_STEP_CHUNKS
    assert l % rows == 0
    pad = LANES - heads
    dtb = jnp.pad(dt_bias.astype(F32), (0, pad)).reshape(1, LANES)
    alog = jnp.pad(a_log.astype(F32), (0, pad)).reshape(1, LANES)
    dskip_e = jnp.repeat(d_skip.astype(F32), HEAD_DIM).reshape(1, d_ssd)
    lane_head = lax.broadcasted_iota(jnp.int32, (LANES, d_ssd), 1) // HEAD_DIM
    row_head = lax.broadcasted_iota(jnp.int32, (LANES, d_ssd), 0)
    expand = jnp.where(lane_head == row_head, 1.0, 0.0).astype(BF16)
    chunk = lambda bi, ci: (bi, ci, 0)
    fixed = lambda bi, ci: (0, 0)
    return pl.pallas_call(
        functools.partial(_ssd_kernel, d_ssd=d_ssd),
        grid=(b, l // rows),
        in_specs=[
            pl.BlockSpec((1, rows, conv_dim), chunk),
            pl.BlockSpec((1, rows, LANES), chunk),
            pl.BlockSpec((1, rows, d_ssd), chunk),
            pl.BlockSpec((CONV_W, conv_dim), fixed),
            pl.BlockSpec((1, conv_dim), fixed),
            pl.BlockSpec((1, LANES), fixed),
            pl.BlockSpec((1, LANES), fixed),
            pl.BlockSpec((1, d_ssd), fixed),
            pl.BlockSpec((1, d_ssd), fixed),
            pl.BlockSpec((LANES, d_ssd), fixed),
        ],
        out_specs=pl.BlockSpec((1, rows, d_ssd), chunk),
        out_shape=jax.ShapeDtypeStruct((b, l, d_ssd), BF16),
        scratch_shapes=[pltpu.VMEM((rows + CONV_HALO, conv_dim), F32),
                        pltpu.VMEM((SSD_STATE, d_ssd), F32)],
        compiler_params=pltpu.CompilerParams(
            dimension_semantics=("arbitrary", "arbitrary"), vmem_limit_bytes=VMEM_LIMIT),
        name="ssd",
    )(xbc, dt_raw, gate, conv_w.astype(F32), conv_b.astype(F32).reshape(1, conv_dim),
      dtb, alog, dskip_e, ssd_norm_w.astype(F32).reshape(1, d_ssd), expand)


def _out_proj_kernel(o_ref, g_ref, y_ref, x_ref, nw_ref, w_ref, out_ref):
    d_sb = o_ref.shape[-1]
    acc = x_ref[...] + jnp.dot(y_ref[...], w_ref[d_sb:, :], preferred_element_type=F32)
    gated = o_ref[...].astype(F32) * g_ref[...].astype(F32)
    ms = jnp.mean(gated * gated, axis=-1, keepdims=True)
    y_sb = (gated * lax.rsqrt(ms + EPS) * nw_ref[...]).astype(BF16)
    out_ref[...] = acc + jnp.dot(y_sb, w_ref[:d_sb, :], preferred_element_type=F32)


def _out_proj(o_sb, gate_sb, y_ssd, x2, sb_norm_w, w_out):
    m, d = x2.shape
    d_sb = o_sb.shape[-1]
    d_ssd = y_ssd.shape[-1]
    row = lambda i: (i, 0)
    fixed = lambda i: (0, 0)
    return pl.pallas_call(
        _out_proj_kernel,
        grid=(m // ROW_TILE,),
        in_specs=[
            pl.BlockSpec((ROW_TILE, d_sb), row),
            pl.BlockSpec((ROW_TILE, d_sb), row),
            pl.BlockSpec((ROW_TILE, d_ssd), row),
            pl.BlockSpec((ROW_TILE, d), row),
            pl.BlockSpec((1, d_sb), fixed),
            pl.BlockSpec(w_out.shape, fixed, pipeline_mode=pl.Buffered(1)),
        ],
        out_specs=pl.BlockSpec((ROW_TILE, d), row),
        out_shape=jax.ShapeDtypeStruct((m, d), F32),
        compiler_params=pltpu.CompilerParams(
            dimension_semantics=("arbitrary",), vmem_limit_bytes=VMEM_LIMIT),
        name="out_proj",
    )(o_sb, gate_sb, y_ssd, x2, sb_norm_w.astype(F32).reshape(1, d_sb), w_out)


def _layer(x, norm_w, w_in, q_norm_w, k_norm_w, conv_w, conv_b, dt_bias, a_log, d_skip,
           sb_norm_w, ssd_norm_w, w_out):
    b, l, d = x.shape
    d_sb = sb_norm_w.shape[0]
    d_ssd = ssd_norm_w.shape[0]
    conv_dim = conv_w.shape[1]
    heads = dt_bias.shape[0]
    main = 4 * d_sb + d_ssd + conv_dim
    assert w_in.shape[1] == main + heads and heads <= LANES
    assert conv_dim == d_ssd + 2 * SSD_GROUPS * SSD_STATE and heads * HEAD_DIM == d_ssd
    m = b * l
    x2 = x.reshape(m, d)
    w_main = w_in[:, :main].astype(BF16)
    w_dt = jnp.pad(w_in[:, main:], ((0, 0), (0, LANES - heads))).astype(BF16)
    q, k, v, gate_sb, gate_ssd, xbc, dt_raw = _in_proj(
        x2, norm_w.astype(F32), w_main, w_dt, d_sb, d_ssd, conv_dim)
    o_sb = _sb_attention(q.reshape(b, l, d_sb), k.reshape(b, l, d_sb), v.reshape(b, l, d_sb),
                         q_norm_w, k_norm_w)
    y_ssd = _ssd(xbc.reshape(b, l, conv_dim), dt_raw.reshape(b, l, LANES),
                 gate_ssd.reshape(b, l, d_ssd), conv_w, conv_b, dt_bias, a_log, d_skip,
                 ssd_norm_w, d_ssd)
    out = _out_proj(o_sb.reshape(m, d_sb), gate_sb, y_ssd.reshape(m, d_ssd), x2, sb_norm_w,
                    w_out.astype(BF16))
    return out.reshape(b, l, d)


def kernel(x, norm_w, w_in, q_norm_w, k_norm_w, conv_w, conv_b, dt_bias, A_log, D_skip,
           sb_norm_w, ssd_norm_w, w_out):
    for layer in range(norm_w.shape[0]):
        x = _layer(x, norm_w[layer], w_in[layer], q_norm_w[layer], k_norm_w[layer],
                   conv_w[layer], conv_b[layer], dt_bias[layer], A_log[layer], D_skip[layer],
                   sb_norm_w[layer], ssd_norm_w[layer], w_out[layer])
    return x
```

```python
import functools
import math

import jax
import jax.numpy as jnp
from jax import lax
from jax.experimental import pallas as pl
from jax.experimental.pallas import tpu as pltpu

F32 = jnp.float32
BF16 = jnp.bfloat16

EPS = 1e-6
LANES = 128
HEAD_DIM = 64
SSD_STATE = 128
SSD_GROUPS = 2
SSD_CHUNK = 128
SSD_STEP_CHUNKS = 4
CONV_W = 4
CONV_HALO = 8

ROW_TILE = 512
OUT_ROW_TILE = 1024
ATT_TILE = 128
ATT_WINDOW = 256
ATT_UNROLL = 16
KEY_NORM_ROWS = 2048
VMEM_LIMIT = 56 * 1024 * 1024
LOG2E = 1.4426950408889634
LOG2_STICK_FLOOR = -150.1
MASKED_LOG = -1e30


def _silu(z):
    return z * (1.0 / (1.0 + jnp.exp(-z)))


def _softplus(z):
    return jnp.maximum(z, 0.0) + jnp.log(1.0 + jnp.exp(-jnp.abs(z)))


def _in_proj_kernel(x_ref, nw_ref, w_ref, wdt_ref,
                    q_ref, k_ref, v_ref, gsb_ref, gssd_ref, xbc_ref, dt_ref):
    x = x_ref[...]
    ms = jnp.mean(x * x, axis=-1, keepdims=True)
    hn = (x * lax.rsqrt(ms + EPS) * nw_ref[...]).astype(BF16)
    start = 0
    for ref, gate in ((q_ref, False), (k_ref, False), (v_ref, False),
                      (gsb_ref, True), (gssd_ref, True), (xbc_ref, False)):
        width = ref.shape[-1]
        p = jnp.dot(hn, w_ref[:, start:start + width], preferred_element_type=F32)
        ref[...] = (_silu(p) if gate else p).astype(ref.dtype)
        start += width
    dt_ref[...] = jnp.dot(hn, wdt_ref[...], preferred_element_type=F32)


def _in_proj(x2, norm_w, w_main, w_dt, d_sb, d_ssd, conv_dim):
    m, d = x2.shape
    widths = (d_sb, d_sb, d_sb, d_sb, d_ssd, conv_dim)
    row = lambda i: (i, 0)
    fixed = lambda i: (0, 0)
    out_shape = [jax.ShapeDtypeStruct((m, w), BF16) for w in widths]
    out_shape.append(jax.ShapeDtypeStruct((m, LANES), F32))
    out_specs = [pl.BlockSpec((ROW_TILE, w), row) for w in widths]
    out_specs.append(pl.BlockSpec((ROW_TILE, LANES), row))
    return pl.pallas_call(
        _in_proj_kernel,
        grid=(m // ROW_TILE,),
        in_specs=[
            pl.BlockSpec((ROW_TILE, d), row),
            pl.BlockSpec((1, d), fixed),
            pl.BlockSpec(w_main.shape, fixed, pipeline_mode=pl.Buffered(1)),
            pl.BlockSpec(w_dt.shape, fixed, pipeline_mode=pl.Buffered(1)),
        ],
        out_specs=out_specs,
        out_shape=out_shape,
        compiler_params=pltpu.CompilerParams(
            dimension_semantics=("arbitrary",), vmem_limit_bytes=VMEM_LIMIT),
        name="in_proj",
    )(x2, norm_w.reshape(1, d), w_main, w_dt)


def _attn_kernel(q_ref, k_ref, v_ref, qnw_ref, knw_ref, u_ref, o_ref, kn_ref, *, seq_len):
    tq, win = ATT_TILE, ATT_WINDOW
    first = lax.broadcasted_iota(jnp.int32, (1, LANES), 1) < HEAD_DIM
    col_minus_row = (lax.broadcasted_iota(jnp.int32, (tq, win), 1)
                     - lax.broadcasted_iota(jnp.int32, (tq, win), 0))

    same_head = (lax.broadcasted_iota(jnp.int32, (LANES, LANES), 0) < HEAD_DIM) == first
    head_ones = jnp.where(same_head, 1.0, 0.0).astype(BF16)

    def head_norm(t, w):
        ss = jnp.dot((t * t).astype(BF16), head_ones, preferred_element_type=F32)
        return t * lax.rsqrt(ss * (1.0 / HEAD_DIM) + EPS) * w

    def norm_keys(i, carry):
        r = pl.multiple_of(i * KEY_NORM_ROWS, KEY_NORM_ROWS)
        kk = k_ref[0, pl.ds(r, KEY_NORM_ROWS), :].astype(F32)
        kn_ref[pl.ds(r, KEY_NORM_ROWS), :] = head_norm(kk, knw_ref[...]).astype(BF16)
        return carry

    lax.fori_loop(0, seq_len // KEY_NORM_ROWS, norm_keys, 0)

    def stage_scores(qm, start, width):
        kb = kn_ref[pl.ds(pl.multiple_of(start, tq), width), :]
        return lax.dot_general(qm, kb, (((1,), (1,)), ((), ())), preferred_element_type=F32)

    def stage_log_terms(z, mask):
        sp = jnp.maximum(z, 0.0) + jnp.log(1.0 + jnp.exp2(-jnp.abs(z))) * LOG2E
        lq = z - sp
        if mask is not None:
            sp = jnp.where(mask, sp, 0.0)
        return sp.astype(BF16), lq, jnp.sum(sp, axis=1, keepdims=True)

    def stage_cumsum(sp_bf):
        width = sp_bf.shape[1]
        return jnp.dot(sp_bf, u_ref[:width, :width], preferred_element_type=F32)

    def stage_weights(lq, rl, mask):
        w = jnp.exp2(lq + rl)
        if mask is not None:
            w = jnp.where(mask, w, 0.0)
        return w.astype(BF16)

    def stage_values(w_bf, start):
        vb = v_ref[0, pl.ds(pl.multiple_of(start, tq), w_bf.shape[1]), :]
        return jnp.dot(w_bf, vb, preferred_element_type=F32)

    def key_window(qm, start, width):
        sp_bf, lq, s = stage_log_terms(stage_scores(qm, start, width), None)
        w_bf = stage_weights(lq, stage_cumsum(sp_bf), None)
        return stage_values(w_bf, start), s

    def log_stick_left(cs):
        return jnp.max(jnp.maximum(cs[0], cs[1]))

    def query_tiles(it, carry):
        pairs = []
        for b in range(ATT_UNROLL):
            t = it * ATT_UNROLL + b
            row0 = t * tq
            qn = head_norm(q_ref[0, pl.ds(pl.multiple_of(row0, tq), tq), :].astype(F32),
                           qnw_ref[...])
            qn = qn * (LOG2E / math.sqrt(HEAD_DIM))
            start1 = (jnp.maximum(t - 1, 0) if b < 1 else t - 1) * tq
            start2 = (jnp.maximum(t - 2, 0) if b < 2 else t - 2) * tq
            mask = col_minus_row < (row0 - start1 if b < 1 else tq)
            for h in range(2):
                qm = jnp.where(first if h == 0 else jnp.logical_not(first), qn, 0.0).astype(BF16)
                pairs.append({"t": t, "qm": qm,
                              "wins": ((start1, win, mask), (start2, tq, None))})
        for step in range(len(pairs) + 4):
            for lag, p in enumerate(pairs):
                stage = step - lag
                if stage == 0:
                    p["z"] = [stage_scores(p["qm"], s, n) for s, n, _ in p["wins"]]
                elif stage == 1:
                    p["log"] = [stage_log_terms(z, w[2]) for z, w in zip(p["z"], p["wins"])]
                elif stage == 2:
                    p["rl"] = [stage_cumsum(t[0]) for t in p["log"]]
                elif stage == 3:
                    p["w"] = [stage_weights(t[1], rl, w[2])
                              for t, rl, w in zip(p["log"], p["rl"], p["wins"])]
                elif stage == 4:
                    p["pv"] = [stage_values(w, win_[0]) for w, win_ in zip(p["w"], p["wins"])]

        def store(t, acc0, acc1):
            row0 = pl.multiple_of(t * tq, tq)
            o_ref[0, pl.ds(row0, tq), :] = jnp.where(first, acc0, acc1).astype(o_ref.dtype)

        tiles = []
        stick = None
        for b in range(ATT_UNROLL):
            t = pairs[2 * b]["t"]
            accs, cs = [], []
            for p in pairs[2 * b:2 * b + 2]:
                c = -p["log"][0][2]
                second = jnp.exp2(c) * p["pv"][1]
                s_second = p["log"][1][2]
                if b < 2:
                    second = jnp.where(t >= 2, second, 0.0)
                    s_second = jnp.where(t >= 2, s_second, 0.0)
                accs.append(p["pv"][0] + second)
                cs.append(c - s_second)
            store(t, accs[0], accs[1])
            tiles.append((t, accs, cs))
            left = jnp.maximum(cs[0], cs[1])
            if b < 3:
                left = jnp.where(t >= 3, left, MASKED_LOG)
            stick = left if stick is None else jnp.maximum(stick, left)

        def cond(state):
            return jnp.logical_and(state[0] >= 0, state[1] > LOG2_STICK_FLOOR)

        @pl.when(jnp.max(stick) > LOG2_STICK_FLOOR)
        def _():
            for b, (t, accs, cs) in enumerate(tiles):
                qms = [pairs[2 * b]["qm"], pairs[2 * b + 1]["qm"]]

                def body(state, qms=qms):
                    j, _, a0, a1, c0, c1 = state
                    accs, cs = [], []
                    for h, (a, c) in enumerate(((a0, c0), (a1, c1))):
                        pv, s = key_window(qms[h], j * tq, tq)
                        accs.append(a + jnp.exp2(c) * pv)
                        cs.append(c - s)
                    return (j - 1, log_stick_left(cs), accs[0], accs[1], cs[0], cs[1])

                state = lax.while_loop(
                    cond, body, (t - 3, log_stick_left(cs), accs[0], accs[1], cs[0], cs[1]))
                store(t, state[2], state[3])
        return carry

    lax.fori_loop(0, seq_len // (tq * ATT_UNROLL), query_tiles, 0)


def _sb_attention(q, k, v, q_norm_w, k_norm_w):
    b, l, d = q.shape
    assert l % KEY_NORM_ROWS == 0 and l % (ATT_TILE * ATT_UNROLL) == 0 and d % LANES == 0
    blk = ATT_WINDOW
    r_id = lax.broadcasted_iota(jnp.int32, (blk, blk), 0)
    c_id = lax.broadcasted_iota(jnp.int32, (blk, blk), 1)
    u_neg = jnp.where(r_id > c_id, -1.0, 0.0).astype(BF16)
    qnw = jnp.tile(q_norm_w.astype(F32), LANES // HEAD_DIM).reshape(1, LANES)
    knw = jnp.tile(k_norm_w.astype(F32), LANES // HEAD_DIM).reshape(1, LANES)
    pair = lambda bi, p: (bi, 0, p)
    fixed = lambda bi, p: (0, 0)
    return pl.pallas_call(
        functools.partial(_attn_kernel, seq_len=l),
        grid=(b, d // LANES),
        in_specs=[
            pl.BlockSpec((1, l, LANES), pair),
            pl.BlockSpec((1, l, LANES), pair),
            pl.BlockSpec((1, l, LANES), pair),
            pl.BlockSpec((1, LANES), fixed),
            pl.BlockSpec((1, LANES), fixed),
            pl.BlockSpec((blk, blk), fixed),
        ],
        out_specs=pl.BlockSpec((1, l, LANES), pair),
        out_shape=jax.ShapeDtypeStruct((b, l, d), BF16),
        scratch_shapes=[pltpu.VMEM((l, LANES), BF16)],
        compiler_params=pltpu.CompilerParams(
            dimension_semantics=("arbitrary", "arbitrary"), vmem_limit_bytes=VMEM_LIMIT),
        name="sb_attn",
    )(q, k, v, qnw, knw, u_neg)


def _ssd_kernel(xbc_ref, dt_ref, g_ref, cw_ref, cb_ref, dtb_ref, alog_ref, dskip_ref, nw_ref,
                expand_ref, y_ref, ext_ref, state_ref, *, d_ssd):
    ci = pl.program_id(1)
    q = SSD_CHUNK
    gw = SSD_STATE
    heads_per_group = d_ssd // HEAD_DIM // SSD_GROUPS
    group_w = heads_per_group * HEAD_DIM
    step_rows = xbc_ref.shape[1]

    @pl.when(ci == 0)
    def _():
        ext_ref[0:CONV_HALO, :] = jnp.zeros((CONV_HALO, ext_ref.shape[1]), F32)
        state_ref[...] = jnp.zeros(state_ref.shape, F32)

    @pl.when(ci > 0)
    def _():
        ext_ref[0:CONV_HALO, :] = ext_ref[step_rows:step_rows + CONV_HALO, :]

    ext_ref[CONV_HALO:, :] = xbc_ref[0].astype(F32)

    def conv_silu(k):
        ext = ext_ref[pl.ds(k * q, q + CONV_HALO), :]
        conv = cb_ref[...] + ext[CONV_HALO:, :] * cw_ref[CONV_W - 1:CONV_W, :]
        for back in range(1, CONV_W):
            tap = CONV_W - 1 - back
            conv = conv + pltpu.roll(ext, back, axis=0)[CONV_HALO:, :] * cw_ref[tap:tap + 1, :]
        return _silu(conv)

    r_id = lax.broadcasted_iota(jnp.int32, (q, q), 0)
    c_id = lax.broadcasted_iota(jnp.int32, (q, q), 1)
    lower = c_id <= r_id
    tri = jnp.where(lower, 1.0, 0.0).astype(BF16)
    first = lax.broadcasted_iota(jnp.int32, (1, LANES), 1) < HEAD_DIM
    neg_a = -jnp.exp(alog_ref[...]) * LOG2E

    def expand(t):
        return jnp.dot(t.astype(BF16), expand_ref[...], preferred_element_type=F32)

    def chunk_local(k, u):
        rows = pl.ds(k * q, q)
        xs = u[:, :d_ssd]
        b_all = u[:, d_ssd:d_ssd + SSD_GROUPS * gw].astype(BF16)
        c_all = u[:, d_ssd + SSD_GROUPS * gw:].astype(BF16)

        dt = _softplus(dt_ref[0, rows, :] + dtb_ref[...])
        d_a = dt * neg_a
        d_a_hi = d_a.astype(BF16)
        d_a_lo = (d_a - d_a_hi.astype(F32)).astype(BF16)
        a_cs = (jnp.dot(tri, d_a_hi, preferred_element_type=F32)
                + jnp.dot(tri, d_a_lo, preferred_element_type=F32))
        a_cs_t = a_cs.T
        a_last = a_cs[q - 1:q, :]
        dt_e = expand(dt)
        fac_e = expand(dt * jnp.exp2(a_last - a_cs))
        ea_e = expand(jnp.exp2(a_cs))
        xdt = xs * dt_e
        xfac = (xs * fac_e).astype(BF16)

        y_parts, cgs, d_states = [], [], []
        for g in range(SSD_GROUPS):
            bg = b_all[:, g * gw:(g + 1) * gw]
            cg = c_all[:, g * gw:(g + 1) * gw]
            cb = lax.dot_general(cg, bg, (((1,), (1,)), ((), ())),
                                 preferred_element_type=F32)
            bg_t = u[:, d_ssd + g * gw:d_ssd + (g + 1) * gw].T.astype(BF16)
            lo = g * group_w
            cgs.append(cg)
            d_states.append(jnp.dot(bg_t, xfac[:, lo:lo + group_w],
                                    preferred_element_type=F32))
            for m in range(heads_per_group // 2):
                base = lo + m * LANES
                x_pair = xdt[:, base:base + LANES]
                y_pair = None
                for h in range(2):
                    kh = base // HEAD_DIM + h
                    seg = a_cs[:, kh:kh + 1] - a_cs_t[kh:kh + 1, :]
                    decay = jnp.exp2(jnp.where(lower, seg, MASKED_LOG))
                    wts = (cb * decay).astype(BF16)
                    x_h = jnp.where(first if h == 0 else jnp.logical_not(first), x_pair, 0.0)
                    part = jnp.dot(wts, x_h.astype(BF16), preferred_element_type=F32)
                    y_pair = part if y_pair is None else y_pair + part
                y_parts.append(y_pair)
        y_local = jnp.concatenate(y_parts, axis=1) + dskip_ref[...] * xs
        return y_local, cgs, d_states, ea_e

    local = []
    u_next = conv_silu(0)
    for k in range(SSD_STEP_CHUNKS):
        u = u_next
        if k + 1 < SSD_STEP_CHUNKS:
            u_next = conv_silu(k + 1)
        local.append(chunk_local(k, u))

    states = [state_ref[:, g * group_w:(g + 1) * group_w] for g in range(SSD_GROUPS)]
    for k, (y_local, cgs, d_states, ea_e) in enumerate(local):
        y_off = jnp.concatenate(
            [jnp.dot(cgs[g], states[g].astype(BF16), preferred_element_type=F32)
             for g in range(SSD_GROUPS)], axis=1) * ea_e
        states = [states[g] * ea_e[q - 1:q, g * group_w:(g + 1) * group_w] + d_states[g]
                  for g in range(SSD_GROUPS)]
        rows = pl.ds(k * q, q)
        gated = (y_local + y_off) * g_ref[0, rows, :].astype(F32)
        ms = jnp.mean(gated * gated, axis=-1, keepdims=True)
        y_ref[0, rows, :] = (gated * lax.rsqrt(ms + EPS) * nw_ref[...]).astype(y_ref.dtype)
    for g in range(SSD_GROUPS):
        state_ref[:, g * group_w:(g + 1) * group_w] = states[g]


def _ssd(xbc, dt_raw, gate, conv_w, conv_b, dt_bias, a_log, d_skip, ssd_norm_w, d_ssd):
    b, l, conv_dim = xbc.shape
    heads = d_ssd // HEAD_DIM
    rows = SSD_CHUNK * SSD_STEP_CHUNKS
    assert l % rows == 0
    pad = LANES - heads
    dtb = jnp.pad(dt_bias.astype(F32), (0, pad)).reshape(1, LANES)
    alog = jnp.pad(a_log.astype(F32), (0, pad)).reshape(1, LANES)
    dskip_e = jnp.repeat(d_skip.astype(F32), HEAD_DIM).reshape(1, d_ssd)
    lane_head = lax.broadcasted_iota(jnp.int32, (LANES, d_ssd), 1) // HEAD_DIM
    row_head = lax.broadcasted_iota(jnp.int32, (LANES, d_ssd), 0)
    expand = jnp.where(lane_head == row_head, 1.0, 0.0).astype(BF16)
    chunk = lambda bi, ci: (bi, ci, 0)
    fixed = lambda bi, ci: (0, 0)
    return pl.pallas_call(
        functools.partial(_ssd_kernel, d_ssd=d_ssd),
        grid=(b, l // rows),
        in_specs=[
            pl.BlockSpec((1, rows, conv_dim), chunk),
            pl.BlockSpec((1, rows, LANES), chunk),
            pl.BlockSpec((1, rows, d_ssd), chunk),
            pl.BlockSpec((CONV_W, conv_dim), fixed),
            pl.BlockSpec((1, conv_dim), fixed),
            pl.BlockSpec((1, LANES), fixed),
            pl.BlockSpec((1, LANES), fixed),
            pl.BlockSpec((1, d_ssd), fixed),
            pl.BlockSpec((1, d_ssd), fixed),
            pl.BlockSpec((LANES, d_ssd), fixed),
        ],
        out_specs=pl.BlockSpec((1, rows, d_ssd), chunk),
        out_shape=jax.ShapeDtypeStruct((b, l, d_ssd), BF16),
        scratch_shapes=[pltpu.VMEM((rows + CONV_HALO, conv_dim), F32),
                        pltpu.VMEM((SSD_STATE, d_ssd), F32)],
        compiler_params=pltpu.CompilerParams(
            dimension_semantics=("arbitrary", "arbitrary"), vmem_limit_bytes=VMEM_LIMIT),
        name="ssd",
    )(xbc, dt_raw, gate, conv_w.astype(F32), conv_b.astype(F32).reshape(1, conv_dim),
      dtb, alog, dskip_e, ssd_norm_w.astype(F32).reshape(1, d_ssd), expand)


def _out_proj_kernel(o_ref, g_ref, y_ref, x_ref, nw_ref, w_ref, out_ref):
    d_sb = o_ref.shape[-1]
    acc = x_ref[...] + jnp.dot(y_ref[...], w_ref[d_sb:, :], preferred_element_type=F32)
    gated = o_ref[...].astype(F32) * g_ref[...].astype(F32)
    ms = jnp.mean(gated * gated, axis=-1, keepdims=True)
    y_sb = (gated * lax.rsqrt(ms + EPS) * nw_ref[...]).astype(BF16)
    out_ref[...] = acc + jnp.dot(y_sb, w_ref[:d_sb, :], preferred_element_type=F32)


def _out_proj(o_sb, gate_sb, y_ssd, x2, sb_norm_w, w_out):
    m, d = x2.shape
    d_sb = o_sb.shape[-1]
    d_ssd = y_ssd.shape[-1]
    row = lambda i: (i, 0)
    fixed = lambda i: (0, 0)
    return pl.pallas_call(
        _out_proj_kernel,
        grid=(m // OUT_ROW_TILE,),
        in_specs=[
            pl.BlockSpec((OUT_ROW_TILE, d_sb), row),
            pl.BlockSpec((OUT_ROW_TILE, d_sb), row),
            pl.BlockSpec((OUT_ROW_TILE, d_ssd), row),
            pl.BlockSpec((OUT_ROW_TILE, d), row),
            pl.BlockSpec((1, d_sb), fixed),
            pl.BlockSpec(w_out.shape, fixed, pipeline_mode=pl.Buffered(1)),
        ],
        out_specs=pl.BlockSpec((OUT_ROW_TILE, d), row),
        out_shape=jax.ShapeDtypeStruct((m, d), F32),
        compiler_params=pltpu.CompilerParams(
            dimension_semantics=("arbitrary",), vmem_limit_bytes=VMEM_LIMIT),
        name="out_proj",
    )(o_sb, gate_sb, y_ssd, x2, sb_norm_w.astype(F32).reshape(1, d_sb), w_out)


def _layer(x, norm_w, w_in, q_norm_w, k_norm_w, conv_w, conv_b, dt_bias, a_log, d_skip,
           sb_norm_w, ssd_norm_w, w_out):
    b, l, d = x.shape
    d_sb = sb_norm_w.shape[0]
    d_ssd = ssd_norm_w.shape[0]
    conv_dim = conv_w.shape[1]
    heads = dt_bias.shape[0]
    main = 4 * d_sb + d_ssd + conv_dim
    assert w_in.shape[1] == main + heads and heads <= LANES
    assert conv_dim == d_ssd + 2 * SSD_GROUPS * SSD_STATE and heads * HEAD_DIM == d_ssd
    m = b * l
    x2 = x.reshape(m, d)
    w_main = w_in[:, :main].astype(BF16)
    w_dt = jnp.pad(w_in[:, main:], ((0, 0), (0, LANES - heads))).astype(BF16)
    q, k, v, gate_sb, gate_ssd, xbc, dt_raw = _in_proj(
        x2, norm_w.astype(F32), w_main, w_dt, d_sb, d_ssd, conv_dim)
    o_sb = _sb_attention(q.reshape(b, l, d_sb), k.reshape(b, l, d_sb), v.reshape(b, l, d_sb),
                         q_norm_w, k_norm_w)
    y_ssd = _ssd(xbc.reshape(b, l, conv_dim), dt_raw.reshape(b, l, LANES),
                 gate_ssd.reshape(b, l, d_ssd), conv_w, conv_b, dt_bias, a_log, d_skip,
                 ssd_norm_w, d_ssd)
    out = _out_proj(o_sb.reshape(m, d_sb), gate_sb, y_ssd.reshape(m, d_ssd), x2, sb_norm_w,
                    w_out.astype(BF16))
    return out.reshape(b, l, d)


def kernel(x, norm_w, w_in, q_norm_w, k_norm_w, conv_w, conv_b, dt_bias, A_log, D_skip,
           sb_norm_w, ssd_norm_w, w_out):
    for layer in range(norm_w.shape[0]):
        x = _layer(x, norm_w[layer], w_in[layer], q_norm_w[layer], k_norm_w[layer],
                   conv_w[layer], conv_b[layer], dt_bias[layer], A_log[layer], D_skip[layer],
                   sb_norm_w[layer], ssd_norm_w[layer], w_out[layer])
    return x
```

```python
import functools
import math

import jax
import jax.numpy as jnp
from jax import lax
from jax.experimental import pallas as pl
from jax.experimental.pallas import tpu as pltpu

F32 = jnp.float32
BF16 = jnp.bfloat16

EPS = 1e-6
LANES = 128
HEAD_DIM = 64
SSD_STATE = 128
SSD_GROUPS = 2
SSD_CHUNK = 128
SSD_STEP_CHUNKS = 4
CONV_W = 4
CONV_HALO = 8

ROW_TILE = 512
OUT_ROW_TILE = 1024
ATT_TILE = 128
ATT_WINDOW = 256
ATT_UNROLL = 32
KEY_NORM_ROWS = 2048
VMEM_LIMIT = 56 * 1024 * 1024
LOG2E = 1.4426950408889634
LOG2_STICK_FLOOR = -150.1
MASKED_LOG = -1e30


def _silu(z):
    return z * (1.0 / (1.0 + jnp.exp(-z)))


def _softplus(z):
    return jnp.maximum(z, 0.0) + jnp.log(1.0 + jnp.exp(-jnp.abs(z)))


def _in_proj_kernel(x_ref, nw_ref, w_ref, wdt_ref,
                    q_ref, k_ref, v_ref, gsb_ref, gssd_ref, xbc_ref, dt_ref):
    x = x_ref[...]
    ms = jnp.mean(x * x, axis=-1, keepdims=True)
    hn = (x * lax.rsqrt(ms + EPS) * nw_ref[...]).astype(BF16)
    start = 0
    for ref, gate in ((q_ref, False), (k_ref, False), (v_ref, False),
                      (gsb_ref, True), (gssd_ref, True), (xbc_ref, False)):
        width = ref.shape[-1]
        p = jnp.dot(hn, w_ref[:, start:start + width], preferred_element_type=F32)
        ref[...] = (_silu(p) if gate else p).astype(ref.dtype)
        start += width
    dt_ref[...] = jnp.dot(hn, wdt_ref[...], preferred_element_type=F32)


def _in_proj(x2, norm_w, w_main, w_dt, d_sb, d_ssd, conv_dim):
    m, d = x2.shape
    widths = (d_sb, d_sb, d_sb, d_sb, d_ssd, conv_dim)
    row = lambda i: (i, 0)
    fixed = lambda i: (0, 0)
    out_shape = [jax.ShapeDtypeStruct((m, w), BF16) for w in widths]
    out_shape.append(jax.ShapeDtypeStruct((m, LANES), F32))
    out_specs = [pl.BlockSpec((ROW_TILE, w), row) for w in widths]
    out_specs.append(pl.BlockSpec((ROW_TILE, LANES), row))
    return pl.pallas_call(
        _in_proj_kernel,
        grid=(m // ROW_TILE,),
        in_specs=[
            pl.BlockSpec((ROW_TILE, d), row),
            pl.BlockSpec((1, d), fixed),
            pl.BlockSpec(w_main.shape, fixed, pipeline_mode=pl.Buffered(1)),
            pl.BlockSpec(w_dt.shape, fixed, pipeline_mode=pl.Buffered(1)),
        ],
        out_specs=out_specs,
        out_shape=out_shape,
        compiler_params=pltpu.CompilerParams(
            dimension_semantics=("arbitrary",), vmem_limit_bytes=VMEM_LIMIT),
        name="in_proj",
    )(x2, norm_w.reshape(1, d), w_main, w_dt)


def _attn_kernel(q_ref, k_ref, v_ref, qnw_ref, knw_ref, u_ref, o_ref, kn_ref, *, seq_len):
    tq, win = ATT_TILE, ATT_WINDOW
    first = lax.broadcasted_iota(jnp.int32, (1, LANES), 1) < HEAD_DIM
    col_minus_row = (lax.broadcasted_iota(jnp.int32, (tq, win), 1)
                     - lax.broadcasted_iota(jnp.int32, (tq, win), 0))

    same_head = (lax.broadcasted_iota(jnp.int32, (LANES, LANES), 0) < HEAD_DIM) == first
    head_ones = jnp.where(same_head, 1.0, 0.0).astype(BF16)

    def head_norm(t, w):
        ss = jnp.dot((t * t).astype(BF16), head_ones, preferred_element_type=F32)
        return t * lax.rsqrt(ss * (1.0 / HEAD_DIM) + EPS) * w

    def norm_keys(i, carry):
        r = pl.multiple_of(i * KEY_NORM_ROWS, KEY_NORM_ROWS)
        kk = k_ref[0, pl.ds(r, KEY_NORM_ROWS), :].astype(F32)
        kn_ref[pl.ds(r, KEY_NORM_ROWS), :] = head_norm(kk, knw_ref[...]).astype(BF16)
        return carry

    lax.fori_loop(0, seq_len // KEY_NORM_ROWS, norm_keys, 0)

    def stage_scores(qm, start, width):
        kb = kn_ref[pl.ds(pl.multiple_of(start, tq), width), :]
        return lax.dot_general(qm, kb, (((1,), (1,)), ((), ())), preferred_element_type=F32)

    def stage_log_terms(z, mask):
        sp = jnp.maximum(z, 0.0) + jnp.log(1.0 + jnp.exp2(-jnp.abs(z))) * LOG2E
        lq = z - sp
        if mask is not None:
            sp = jnp.where(mask, sp, 0.0)
        return sp.astype(BF16), lq, jnp.sum(sp, axis=1, keepdims=True)

    def stage_cumsum(sp_bf):
        width = sp_bf.shape[1]
        return jnp.dot(sp_bf, u_ref[:width, :width], preferred_element_type=F32)

    def stage_weights(lq, rl, mask):
        w = jnp.exp2(lq + rl)
        if mask is not None:
            w = jnp.where(mask, w, 0.0)
        return w.astype(BF16)

    def stage_values(w_bf, start):
        vb = v_ref[0, pl.ds(pl.multiple_of(start, tq), w_bf.shape[1]), :]
        return jnp.dot(w_bf, vb, preferred_element_type=F32)

    def key_window(qm, start, width):
        sp_bf, lq, s = stage_log_terms(stage_scores(qm, start, width), None)
        w_bf = stage_weights(lq, stage_cumsum(sp_bf), None)
        return stage_values(w_bf, start), s

    def log_stick_left(cs):
        return jnp.max(jnp.maximum(cs[0], cs[1]))

    def query_tiles(it, carry):
        pairs = []
        for b in range(ATT_UNROLL):
            t = it * ATT_UNROLL + b
            row0 = t * tq
            qn = head_norm(q_ref[0, pl.ds(pl.multiple_of(row0, tq), tq), :].astype(F32),
                           qnw_ref[...])
            qn = qn * (LOG2E / math.sqrt(HEAD_DIM))
            start1 = (jnp.maximum(t - 1, 0) if b < 1 else t - 1) * tq
            start2 = (jnp.maximum(t - 2, 0) if b < 2 else t - 2) * tq
            mask = col_minus_row < (row0 - start1 if b < 1 else tq)
            for h in range(2):
                qm = jnp.where(first if h == 0 else jnp.logical_not(first), qn, 0.0).astype(BF16)
                pairs.append({"t": t, "qm": qm,
                              "wins": ((start1, win, mask), (start2, tq, None))})
        for step in range(len(pairs) + 4):
            for lag, p in enumerate(pairs):
                stage = step - lag
                if stage == 0:
                    p["z"] = [stage_scores(p["qm"], s, n) for s, n, _ in p["wins"]]
                elif stage == 1:
                    p["log"] = [stage_log_terms(z, w[2]) for z, w in zip(p["z"], p["wins"])]
                elif stage == 2:
                    p["rl"] = [stage_cumsum(t[0]) for t in p["log"]]
                elif stage == 3:
                    p["w"] = [stage_weights(t[1], rl, w[2])
                              for t, rl, w in zip(p["log"], p["rl"], p["wins"])]
                elif stage == 4:
                    p["pv"] = [stage_values(w, win_[0]) for w, win_ in zip(p["w"], p["wins"])]

        def store(t, acc0, acc1):
            row0 = pl.multiple_of(t * tq, tq)
            o_ref[0, pl.ds(row0, tq), :] = jnp.where(first, acc0, acc1).astype(o_ref.dtype)

        tiles = []
        stick = None
        for b in range(ATT_UNROLL):
            t = pairs[2 * b]["t"]
            accs, cs = [], []
            for p in pairs[2 * b:2 * b + 2]:
                c = -p["log"][0][2]
                second = jnp.exp2(c) * p["pv"][1]
                s_second = p["log"][1][2]
                if b < 2:
                    second = jnp.where(t >= 2, second, 0.0)
                    s_second = jnp.where(t >= 2, s_second, 0.0)
                accs.append(p["pv"][0] + second)
                cs.append(c - s_second)
            store(t, accs[0], accs[1])
            tiles.append((t, accs, cs))
            left = jnp.maximum(cs[0], cs[1])
            if b < 3:
                left = jnp.where(t >= 3, left, MASKED_LOG)
            stick = left if stick is None else jnp.maximum(stick, left)

        def cond(state):
            return jnp.logical_and(state[0] >= 0, state[1] > LOG2_STICK_FLOOR)

        @pl.when(jnp.max(stick) > LOG2_STICK_FLOOR)
        def _():
            for b, (t, accs, cs) in enumerate(tiles):
                qms = [pairs[2 * b]["qm"], pairs[2 * b + 1]["qm"]]

                def body(state, qms=qms):
                    j, _, a0, a1, c0, c1 = state
                    accs, cs = [], []
                    for h, (a, c) in enumerate(((a0, c0), (a1, c1))):
                        pv, s = key_window(qms[h], j * tq, tq)
                        accs.append(a + jnp.exp2(c) * pv)
                        cs.append(c - s)
                    return (j - 1, log_stick_left(cs), accs[0], accs[1], cs[0], cs[1])

                state = lax.while_loop(
                    cond, body, (t - 3, log_stick_left(cs), accs[0], accs[1], cs[0], cs[1]))
                store(t, state[2], state[3])
        return carry

    lax.fori_loop(0, seq_len // (tq * ATT_UNROLL), query_tiles, 0)


def _sb_attention(q, k, v, q_norm_w, k_norm_w):
    b, l, d = q.shape
    assert l % KEY_NORM_ROWS == 0 and l % (ATT_TILE * ATT_UNROLL) == 0 and d % LANES == 0
    blk = ATT_WINDOW
    r_id = lax.broadcasted_iota(jnp.int32, (blk, blk), 0)
    c_id = lax.broadcasted_iota(jnp.int32, (blk, blk), 1)
    u_neg = jnp.where(r_id > c_id, -1.0, 0.0).astype(BF16)
    qnw = jnp.tile(q_norm_w.astype(F32), LANES // HEAD_DIM).reshape(1, LANES)
    knw = jnp.tile(k_norm_w.astype(F32), LANES // HEAD_DIM).reshape(1, LANES)
    pair = lambda bi, p: (bi, 0, p)
    fixed = lambda bi, p: (0, 0)
    return pl.pallas_call(
        functools.partial(_attn_kernel, seq_len=l),
        grid=(b, d // LANES),
        in_specs=[
            pl.BlockSpec((1, l, LANES), pair),
            pl.BlockSpec((1, l, LANES), pair),
            pl.BlockSpec((1, l, LANES), pair),
            pl.BlockSpec((1, LANES), fixed),
            pl.BlockSpec((1, LANES), fixed),
            pl.BlockSpec((blk, blk), fixed),
        ],
        out_specs=pl.BlockSpec((1, l, LANES), pair),
        out_shape=jax.ShapeDtypeStruct((b, l, d), BF16),
        scratch_shapes=[pltpu.VMEM((l, LANES), BF16)],
        compiler_params=pltpu.CompilerParams(
            dimension_semantics=("arbitrary", "arbitrary"), vmem_limit_bytes=VMEM_LIMIT),
        name="sb_attn",
    )(q, k, v, qnw, knw, u_neg)


def _ssd_kernel(xbc_ref, dt_ref, g_ref, cw_ref, cb_ref, dtb_ref, alog_ref, dskip_ref, nw_ref,
                expand_ref, y_ref, ext_ref, state_ref, *, d_ssd):
    ci = pl.program_id(1)
    q = SSD_CHUNK
    gw = SSD_STATE
    heads_per_group = d_ssd // HEAD_DIM // SSD_GROUPS
    group_w = heads_per_group * HEAD_DIM
    step_rows = xbc_ref.shape[1]

    @pl.when(ci == 0)
    def _():
        ext_ref[0:CONV_HALO, :] = jnp.zeros((CONV_HALO, ext_ref.shape[1]), F32)
        state_ref[...] = jnp.zeros(state_ref.shape, F32)

    @pl.when(ci > 0)
    def _():
        ext_ref[0:CONV_HALO, :] = ext_ref[step_rows:step_rows + CONV_HALO, :]

    ext_ref[CONV_HALO:, :] = xbc_ref[0].astype(F32)

    def conv_silu(k):
        ext = ext_ref[pl.ds(k * q, q + CONV_HALO), :]
        conv = cb_ref[...] + ext[CONV_HALO:, :] * cw_ref[CONV_W - 1:CONV_W, :]
        for back in range(1, CONV_W):
            tap = CONV_W - 1 - back
            conv = conv + pltpu.roll(ext, back, axis=0)[CONV_HALO:, :] * cw_ref[tap:tap + 1, :]
        return _silu(conv)

    r_id = lax.broadcasted_iota(jnp.int32, (q, q), 0)
    c_id = lax.broadcasted_iota(jnp.int32, (q, q), 1)
    lower = c_id <= r_id
    tri = jnp.where(lower, 1.0, 0.0).astype(BF16)
    first = lax.broadcasted_iota(jnp.int32, (1, LANES), 1) < HEAD_DIM
    neg_a = -jnp.exp(alog_ref[...]) * LOG2E

    def expand(t):
        return jnp.dot(t.astype(BF16), expand_ref[...], preferred_element_type=F32)

    def chunk_local(k, u):
        rows = pl.ds(k * q, q)
        xs = u[:, :d_ssd]
        b_all = u[:, d_ssd:d_ssd + SSD_GROUPS * gw].astype(BF16)
        c_all = u[:, d_ssd + SSD_GROUPS * gw:].astype(BF16)

        dt = _softplus(dt_ref[0, rows, :] + dtb_ref[...])
        d_a = dt * neg_a
        d_a_hi = d_a.astype(BF16)
        d_a_lo = (d_a - d_a_hi.astype(F32)).astype(BF16)
        a_cs = (jnp.dot(tri, d_a_hi, preferred_element_type=F32)
                + jnp.dot(tri, d_a_lo, preferred_element_type=F32))
        a_cs_t = a_cs.T
        a_last = a_cs[q - 1:q, :]
        dt_e = expand(dt)
        fac_e = expand(dt * jnp.exp2(a_last - a_cs))
        ea_e = expand(jnp.exp2(a_cs))
        xdt = xs * dt_e
        xfac = (xs * fac_e).astype(BF16)

        y_parts, cgs, d_states = [], [], []
        for g in range(SSD_GROUPS):
            bg = b_all[:, g * gw:(g + 1) * gw]
            cg = c_all[:, g * gw:(g + 1) * gw]
            cb = lax.dot_general(cg, bg, (((1,), (1,)), ((), ())),
                                 preferred_element_type=F32)
            bg_t = u[:, d_ssd + g * gw:d_ssd + (g + 1) * gw].T.astype(BF16)
            lo = g * group_w
            cgs.append(cg)
            d_states.append(jnp.dot(bg_t, xfac[:, lo:lo + group_w],
                                    preferred_element_type=F32))
            for m in range(heads_per_group // 2):
                base = lo + m * LANES
                x_pair = xdt[:, base:base + LANES]
                y_pair = None
                for h in range(2):
                    kh = base // HEAD_DIM + h
                    seg = a_cs[:, kh:kh + 1] - a_cs_t[kh:kh + 1, :]
                    decay = jnp.exp2(jnp.where(lower, seg, MASKED_LOG))
                    wts = (cb * decay).astype(BF16)
                    x_h = jnp.where(first if h == 0 else jnp.logical_not(first), x_pair, 0.0)
                    part = jnp.dot(wts, x_h.astype(BF16), preferred_element_type=F32)
                    y_pair = part if y_pair is None else y_pair + part
                y_parts.append(y_pair)
        y_local = jnp.concatenate(y_parts, axis=1) + dskip_ref[...] * xs
        return y_local, cgs, d_states, ea_e

    local = []
    u_next = conv_silu(0)
    for k in range(SSD_STEP_CHUNKS):
        u = u_next
        if k + 1 < SSD_STEP_CHUNKS:
            u_next = conv_silu(k + 1)
        local.append(chunk_local(k, u))

    states = [state_ref[:, g * group_w:(g + 1) * group_w] for g in range(SSD_GROUPS)]
    for k, (y_local, cgs, d_states, ea_e) in enumerate(local):
        y_off = jnp.concatenate(
            [jnp.dot(cgs[g], states[g].astype(BF16), preferred_element_type=F32)
             for g in range(SSD_GROUPS)], axis=1) * ea_e
        states = [states[g] * ea_e[q - 1:q, g * group_w:(g + 1) * group_w] + d_states[g]
                  for g in range(SSD_GROUPS)]
        rows = pl.ds(k * q, q)
        gated = (y_local + y_off) * g_ref[0, rows, :].astype(F32)
        ms = jnp.mean(gated * gated, axis=-1, keepdims=True)
        y_ref[0, rows, :] = (gated * lax.rsqrt(ms + EPS) * nw_ref[...]).astype(y_ref.dtype)
    for g in range(SSD_GROUPS):
        state_ref[:, g * group_w:(g + 1) * group_w] = states[g]


def _ssd(xbc, dt_raw, gate, conv_w, conv_b, dt_bias, a_log, d_skip, ssd_norm_w, d_ssd):
    b, l, conv_dim = xbc.shape
    heads = d_ssd // HEAD_DIM
    rows = SSD_CHUNK * SSD_STEP_CHUNKS
    assert l % rows == 0
    pad = LANES - heads
    dtb = jnp.pad(dt_bias.astype(F32), (0, pad)).reshape(1, LANES)
    alog = jnp.pad(a_log.astype(F32), (0, pad)).reshape(1, LANES)
    dskip_e = jnp.repeat(d_skip.astype(F32), HEAD_DIM).reshape(1, d_ssd)
    lane_head = lax.broadcasted_iota(jnp.int32, (LANES, d_ssd), 1) // HEAD_DIM
    row_head = lax.broadcasted_iota(jnp.int32, (LANES, d_ssd), 0)
    expand = jnp.where(lane_head == row_head, 1.0, 0.0).astype(BF16)
    chunk = lambda bi, ci: (bi, ci, 0)
    fixed = lambda bi, ci: (0, 0)
    return pl.pallas_call(
        functools.partial(_ssd_kernel, d_ssd=d_ssd),
        grid=(b, l // rows),
        in_specs=[
            pl.BlockSpec((1, rows, conv_dim), chunk),
            pl.BlockSpec((1, rows, LANES), chunk),
            pl.BlockSpec((1, rows, d_ssd), chunk),
            pl.BlockSpec((CONV_W, conv_dim), fixed),
            pl.BlockSpec((1, conv_dim), fixed),
            pl.BlockSpec((1, LANES), fixed),
            pl.BlockSpec((1, LANES), fixed),
            pl.BlockSpec((1, d_ssd), fixed),
            pl.BlockSpec((1, d_ssd), fixed),
            pl.BlockSpec((LANES, d_ssd), fixed),
        ],
        out_specs=pl.BlockSpec((1, rows, d_ssd), chunk),
        out_shape=jax.ShapeDtypeStruct((b, l, d_ssd), BF16),
        scratch_shapes=[pltpu.VMEM((rows + CONV_HALO, conv_dim), F32),
                        pltpu.VMEM((SSD_STATE, d_ssd), F32)],
        compiler_params=pltpu.CompilerParams(
            dimension_semantics=("arbitrary", "arbitrary"), vmem_limit_bytes=VMEM_LIMIT),
        name="ssd",
    )(xbc, dt_raw, gate, conv_w.astype(F32), conv_b.astype(F32).reshape(1, conv_dim),
      dtb, alog, dskip_e, ssd_norm_w.astype(F32).reshape(1, d_ssd), expand)


def _out_proj_kernel(o_ref, g_ref, y_ref, x_ref, nw_ref, w_ref, out_ref):
    d_sb = o_ref.shape[-1]
    acc = x_ref[...] + jnp.dot(y_ref[...], w_ref[d_sb:, :], preferred_element_type=F32)
    gated = o_ref[...].astype(F32) * g_ref[...].astype(F32)
    ms = jnp.mean(gated * gated, axis=-1, keepdims=True)
    y_sb = (gated * lax.rsqrt(ms + EPS) * nw_ref[...]).astype(BF16)
    out_ref[...] = acc + jnp.dot(y_sb, w_ref[:d_sb, :], preferred_element_type=F32)


def _out_proj(o_sb, gate_sb, y_ssd, x2, sb_norm_w, w_out):
    m, d = x2.shape
    d_sb = o_sb.shape[-1]
    d_ssd = y_ssd.shape[-1]
    row = lambda i: (i, 0)
    fixed = lambda i: (0, 0)
    return pl.pallas_call(
        _out_proj_kernel,
        grid=(m // OUT_ROW_TILE,),
        in_specs=[
            pl.BlockSpec((OUT_ROW_TILE, d_sb), row),
            pl.BlockSpec((OUT_ROW_TILE, d_sb), row),
            pl.BlockSpec((OUT_ROW_TILE, d_ssd), row),
            pl.BlockSpec((OUT_ROW_TILE, d), row),
            pl.BlockSpec((1, d_sb), fixed),
            pl.BlockSpec(w_out.shape, fixed, pipeline_mode=pl.Buffered(1)),
        ],
        out_specs=pl.BlockSpec((OUT_ROW_TILE, d), row),
        out_shape=jax.ShapeDtypeStruct((m, d), F32),
        compiler_params=pltpu.CompilerParams(
            dimension_semantics=("arbitrary",), vmem_limit_bytes=VMEM_LIMIT),
        name="out_proj",
    )(o_sb, gate_sb, y_ssd, x2, sb_norm_w.astype(F32).reshape(1, d_sb), w_out)


def _layer(x, norm_w, w_in, q_norm_w, k_norm_w, conv_w, conv_b, dt_bias, a_log, d_skip,
           sb_norm_w, ssd_norm_w, w_out):
    b, l, d = x.shape
    d_sb = sb_norm_w.shape[0]
    d_ssd = ssd_norm_w.shape[0]
    conv_dim = conv_w.shape[1]
    heads = dt_bias.shape[0]
    main = 4 * d_sb + d_ssd + conv_dim
    assert w_in.shape[1] == main + heads and heads <= LANES
    assert conv_dim == d_ssd + 2 * SSD_GROUPS * SSD_STATE and heads * HEAD_DIM == d_ssd
    m = b * l
    x2 = x.reshape(m, d)
    w_main = w_in[:, :main].astype(BF16)
    w_dt = jnp.pad(w_in[:, main:], ((0, 0), (0, LANES - heads))).astype(BF16)
    q, k, v, gate_sb, gate_ssd, xbc, dt_raw = _in_proj(
        x2, norm_w.astype(F32), w_main, w_dt, d_sb, d_ssd, conv_dim)
    o_sb = _sb_attention(q.reshape(b, l, d_sb), k.reshape(b, l, d_sb), v.reshape(b, l, d_sb),
                         q_norm_w, k_norm_w)
    y_ssd = _ssd(xbc.reshape(b, l, conv_dim), dt_raw.reshape(b, l, LANES),
                 gate_ssd.reshape(b, l, d_ssd), conv_w, conv_b, dt_bias, a_log, d_skip,
                 ssd_norm_w, d_ssd)
    out = _out_proj(o_sb.reshape(m, d_sb), gate_sb, y_ssd.reshape(m, d_ssd), x2, sb_norm_w,
                    w_out.astype(BF16))
    return out.reshape(b, l, d)


def kernel(x, norm_w, w_in, q_norm_w, k_norm_w, conv_w, conv_b, dt_bias, A_log, D_skip,
           sb_norm_w, ssd_norm_w, w_out):
    for layer in range(norm_w.shape[0]):
        x = _layer(x, norm_w[layer], w_in[layer], q_norm_w[layer], k_norm_w[layer],
                   conv_w[layer], conv_b[layer], dt_bias[layer], A_log[layer], D_skip[layer],
                   sb_norm_w[layer], ssd_norm_w[layer], w_out[layer])
    return x
```

```python
import functools
import math

import jax
import jax.numpy as jnp
from jax import lax
from jax.experimental import pallas as pl
from jax.experimental.pallas import tpu as pltpu

F32 = jnp.float32
BF16 = jnp.bfloat16

EPS = 1e-6
LANES = 128
HEAD_DIM = 64
SSD_STATE = 128
SSD_GROUPS = 2
SSD_CHUNK = 128
SSD_STEP_CHUNKS = 8
CONV_W = 4
CONV_HALO = 8

ROW_TILE = 512
OUT_ROW_TILE = 1024
ATT_TILE = 128
ATT_WINDOW = 256
ATT_UNROLL = 32
KEY_NORM_ROWS = 2048
VMEM_LIMIT = 56 * 1024 * 1024
LOG2E = 1.4426950408889634
LOG2_STICK_FLOOR = -150.1
MASKED_LOG = -1e30


def _silu(z):
    return z * (1.0 / (1.0 + jnp.exp(-z)))


def _softplus(z):
    return jnp.maximum(z, 0.0) + jnp.log(1.0 + jnp.exp(-jnp.abs(z)))


def _in_proj_kernel(x_ref, nw_ref, w_ref, wdt_ref,
                    q_ref, k_ref, v_ref, gsb_ref, gssd_ref, xbc_ref, dt_ref):
    x = x_ref[...]
    ms = jnp.mean(x * x, axis=-1, keepdims=True)
    hn = (x * lax.rsqrt(ms + EPS) * nw_ref[...]).astype(BF16)
    start = 0
    for ref, gate in ((q_ref, False), (k_ref, False), (v_ref, False),
                      (gsb_ref, True), (gssd_ref, True), (xbc_ref, False)):
        width = ref.shape[-1]
        p = jnp.dot(hn, w_ref[:, start:start + width], preferred_element_type=F32)
        ref[...] = (_silu(p) if gate else p).astype(ref.dtype)
        start += width
    dt_ref[...] = jnp.dot(hn, wdt_ref[...], preferred_element_type=F32)


def _in_proj(x2, norm_w, w_main, w_dt, d_sb, d_ssd, conv_dim):
    m, d = x2.shape
    widths = (d_sb, d_sb, d_sb, d_sb, d_ssd, conv_dim)
    row = lambda i: (i, 0)
    fixed = lambda i: (0, 0)
    out_shape = [jax.ShapeDtypeStruct((m, w), BF16) for w in widths]
    out_shape.append(jax.ShapeDtypeStruct((m, LANES), F32))
    out_specs = [pl.BlockSpec((ROW_TILE, w), row) for w in widths]
    out_specs.append(pl.BlockSpec((ROW_TILE, LANES), row))
    return pl.pallas_call(
        _in_proj_kernel,
        grid=(m // ROW_TILE,),
        in_specs=[
            pl.BlockSpec((ROW_TILE, d), row),
            pl.BlockSpec((1, d), fixed),
            pl.BlockSpec(w_main.shape, fixed, pipeline_mode=pl.Buffered(1)),
            pl.BlockSpec(w_dt.shape, fixed, pipeline_mode=pl.Buffered(1)),
        ],
        out_specs=out_specs,
        out_shape=out_shape,
        compiler_params=pltpu.CompilerParams(
            dimension_semantics=("arbitrary",), vmem_limit_bytes=VMEM_LIMIT),
        name="in_proj",
    )(x2, norm_w.reshape(1, d), w_main, w_dt)


def _attn_kernel(q_ref, k_ref, v_ref, qnw_ref, knw_ref, u_ref, o_ref, kn_ref, *, seq_len):
    tq, win = ATT_TILE, ATT_WINDOW
    first = lax.broadcasted_iota(jnp.int32, (1, LANES), 1) < HEAD_DIM
    col_minus_row = (lax.broadcasted_iota(jnp.int32, (tq, win), 1)
                     - lax.broadcasted_iota(jnp.int32, (tq, win), 0))

    same_head = (lax.broadcasted_iota(jnp.int32, (LANES, LANES), 0) < HEAD_DIM) == first
    head_ones = jnp.where(same_head, 1.0, 0.0).astype(BF16)

    def head_norm(t, w):
        ss = jnp.dot((t * t).astype(BF16), head_ones, preferred_element_type=F32)
        return t * lax.rsqrt(ss * (1.0 / HEAD_DIM) + EPS) * w

    def norm_keys(i, carry):
        r = pl.multiple_of(i * KEY_NORM_ROWS, KEY_NORM_ROWS)
        kk = k_ref[0, pl.ds(r, KEY_NORM_ROWS), :].astype(F32)
        kn_ref[pl.ds(r, KEY_NORM_ROWS), :] = head_norm(kk, knw_ref[...]).astype(BF16)
        return carry

    lax.fori_loop(0, seq_len // KEY_NORM_ROWS, norm_keys, 0)

    def stage_scores(qm, start, width):
        kb = kn_ref[pl.ds(pl.multiple_of(start, tq), width), :]
        return lax.dot_general(qm, kb, (((1,), (1,)), ((), ())), preferred_element_type=F32)

    def stage_log_terms(z, mask):
        sp = jnp.maximum(z, 0.0) + jnp.log(1.0 + jnp.exp2(-jnp.abs(z))) * LOG2E
        lq = z - sp
        if mask is not None:
            sp = jnp.where(mask, sp, 0.0)
        return sp.astype(BF16), lq, jnp.sum(sp, axis=1, keepdims=True)

    def stage_cumsum(sp_bf):
        width = sp_bf.shape[1]
        return jnp.dot(sp_bf, u_ref[:width, :width], preferred_element_type=F32)

    def stage_weights(lq, rl, mask):
        w = jnp.exp2(lq + rl)
        if mask is not None:
            w = jnp.where(mask, w, 0.0)
        return w.astype(BF16)

    def stage_values(w_bf, start):
        vb = v_ref[0, pl.ds(pl.multiple_of(start, tq), w_bf.shape[1]), :]
        return jnp.dot(w_bf, vb, preferred_element_type=F32)

    def key_window(qm, start, width):
        sp_bf, lq, s = stage_log_terms(stage_scores(qm, start, width), None)
        w_bf = stage_weights(lq, stage_cumsum(sp_bf), None)
        return stage_values(w_bf, start), s

    def log_stick_left(cs):
        return jnp.max(jnp.maximum(cs[0], cs[1]))

    def query_tiles(it, carry):
        pairs = []
        for b in range(ATT_UNROLL):
            t = it * ATT_UNROLL + b
            row0 = t * tq
            qn = head_norm(q_ref[0, pl.ds(pl.multiple_of(row0, tq), tq), :].astype(F32),
                           qnw_ref[...])
            qn = qn * (LOG2E / math.sqrt(HEAD_DIM))
            start1 = (jnp.maximum(t - 1, 0) if b < 1 else t - 1) * tq
            start2 = (jnp.maximum(t - 2, 0) if b < 2 else t - 2) * tq
            mask = col_minus_row < (row0 - start1 if b < 1 else tq)
            for h in range(2):
                qm = jnp.where(first if h == 0 else jnp.logical_not(first), qn, 0.0).astype(BF16)
                pairs.append({"t": t, "qm": qm,
                              "wins": ((start1, win, mask), (start2, tq, None))})
        for step in range(len(pairs) + 4):
            for lag, p in enumerate(pairs):
                stage = step - lag
                if stage == 0:
                    p["z"] = [stage_scores(p["qm"], s, n) for s, n, _ in p["wins"]]
                elif stage == 1:
                    p["log"] = [stage_log_terms(z, w[2]) for z, w in zip(p["z"], p["wins"])]
                elif stage == 2:
                    p["rl"] = [stage_cumsum(t[0]) for t in p["log"]]
                elif stage == 3:
                    p["w"] = [stage_weights(t[1], rl, w[2])
                              for t, rl, w in zip(p["log"], p["rl"], p["wins"])]
                elif stage == 4:
                    p["pv"] = [stage_values(w, win_[0]) for w, win_ in zip(p["w"], p["wins"])]

        def store(t, acc0, acc1):
            row0 = pl.multiple_of(t * tq, tq)
            o_ref[0, pl.ds(row0, tq), :] = jnp.where(first, acc0, acc1).astype(o_ref.dtype)

        tiles = []
        stick = None
        for b in range(ATT_UNROLL):
            t = pairs[2 * b]["t"]
            accs, cs = [], []
            for p in pairs[2 * b:2 * b + 2]:
                c = -p["log"][0][2]
                second = jnp.exp2(c) * p["pv"][1]
                s_second = p["log"][1][2]
                if b < 2:
                    second = jnp.where(t >= 2, second, 0.0)
                    s_second = jnp.where(t >= 2, s_second, 0.0)
                accs.append(p["pv"][0] + second)
                cs.append(c - s_second)
            store(t, accs[0], accs[1])
            tiles.append((t, accs, cs))
            left = jnp.maximum(cs[0], cs[1])
            if b < 3:
                left = jnp.where(t >= 3, left, MASKED_LOG)
            stick = left if stick is None else jnp.maximum(stick, left)

        def cond(state):
            return jnp.logical_and(state[0] >= 0, state[1] > LOG2_STICK_FLOOR)

        @pl.when(jnp.max(stick) > LOG2_STICK_FLOOR)
        def _():
            for b, (t, accs, cs) in enumerate(tiles):
                qms = [pairs[2 * b]["qm"], pairs[2 * b + 1]["qm"]]

                def body(state, qms=qms):
                    j, _, a0, a1, c0, c1 = state
                    accs, cs = [], []
                    for h, (a, c) in enumerate(((a0, c0), (a1, c1))):
                        pv, s = key_window(qms[h], j * tq, tq)
                        accs.append(a + jnp.exp2(c) * pv)
                        cs.append(c - s)
                    return (j - 1, log_stick_left(cs), accs[0], accs[1], cs[0], cs[1])

                state = lax.while_loop(
                    cond, body, (t - 3, log_stick_left(cs), accs[0], accs[1], cs[0], cs[1]))
                store(t, state[2], state[3])
        return carry

    lax.fori_loop(0, seq_len // (tq * ATT_UNROLL), query_tiles, 0)


def _sb_attention(q, k, v, q_norm_w, k_norm_w):
    b, l, d = q.shape
    assert l % KEY_NORM_ROWS == 0 and l % (ATT_TILE * ATT_UNROLL) == 0 and d % LANES == 0
    blk = ATT_WINDOW
    r_id = lax.broadcasted_iota(jnp.int32, (blk, blk), 0)
    c_id = lax.broadcasted_iota(jnp.int32, (blk, blk), 1)
    u_neg = jnp.where(r_id > c_id, -1.0, 0.0).astype(BF16)
    qnw = jnp.tile(q_norm_w.astype(F32), LANES // HEAD_DIM).reshape(1, LANES)
    knw = jnp.tile(k_norm_w.astype(F32), LANES // HEAD_DIM).reshape(1, LANES)
    pair = lambda bi, p: (bi, 0, p)
    fixed = lambda bi, p: (0, 0)
    return pl.pallas_call(
        functools.partial(_attn_kernel, seq_len=l),
        grid=(b, d // LANES),
        in_specs=[
            pl.BlockSpec((1, l, LANES), pair),
            pl.BlockSpec((1, l, LANES), pair),
            pl.BlockSpec((1, l, LANES), pair),
            pl.BlockSpec((1, LANES), fixed),
            pl.BlockSpec((1, LANES), fixed),
            pl.BlockSpec((blk, blk), fixed),
        ],
        out_specs=pl.BlockSpec((1, l, LANES), pair),
        out_shape=jax.ShapeDtypeStruct((b, l, d), BF16),
        scratch_shapes=[pltpu.VMEM((l, LANES), BF16)],
        compiler_params=pltpu.CompilerParams(
            dimension_semantics=("arbitrary", "arbitrary"), vmem_limit_bytes=VMEM_LIMIT),
        name="sb_attn",
    )(q, k, v, qnw, knw, u_neg)


def _ssd_kernel(xbc_ref, dt_ref, g_ref, cw_ref, cb_ref, dtb_ref, alog_ref, dskip_ref, nw_ref,
                expand_ref, y_ref, ext_ref, state_ref, *, d_ssd):
    ci = pl.program_id(1)
    q = SSD_CHUNK
    gw = SSD_STATE
    heads_per_group = d_ssd // HEAD_DIM // SSD_GROUPS
    group_w = heads_per_group * HEAD_DIM
    step_rows = xbc_ref.shape[1]

    @pl.when(ci == 0)
    def _():
        ext_ref[0:CONV_HALO, :] = jnp.zeros((CONV_HALO, ext_ref.shape[1]), F32)
        state_ref[...] = jnp.zeros(state_ref.shape, F32)

    @pl.when(ci > 0)
    def _():
        ext_ref[0:CONV_HALO, :] = ext_ref[step_rows:step_rows + CONV_HALO, :]

    ext_ref[CONV_HALO:, :] = xbc_ref[0].astype(F32)

    def conv_silu(k):
        ext = ext_ref[pl.ds(k * q, q + CONV_HALO), :]
        conv = cb_ref[...] + ext[CONV_HALO:, :] * cw_ref[CONV_W - 1:CONV_W, :]
        for back in range(1, CONV_W):
            tap = CONV_W - 1 - back
            conv = conv + pltpu.roll(ext, back, axis=0)[CONV_HALO:, :] * cw_ref[tap:tap + 1, :]
        return _silu(conv)

    r_id = lax.broadcasted_iota(jnp.int32, (q, q), 0)
    c_id = lax.broadcasted_iota(jnp.int32, (q, q), 1)
    lower = c_id <= r_id
    tri = jnp.where(lower, 1.0, 0.0).astype(BF16)
    first = lax.broadcasted_iota(jnp.int32, (1, LANES), 1) < HEAD_DIM
    neg_a = -jnp.exp(alog_ref[...]) * LOG2E

    def expand(t):
        return jnp.dot(t.astype(BF16), expand_ref[...], preferred_element_type=F32)

    def chunk_local(k, u):
        rows = pl.ds(k * q, q)
        xs = u[:, :d_ssd]
        b_all = u[:, d_ssd:d_ssd + SSD_GROUPS * gw].astype(BF16)
        c_all = u[:, d_ssd + SSD_GROUPS * gw:].astype(BF16)

        dt = _softplus(dt_ref[0, rows, :] + dtb_ref[...])
        d_a = dt * neg_a
        d_a_hi = d_a.astype(BF16)
        d_a_lo = (d_a - d_a_hi.astype(F32)).astype(BF16)
        a_cs = (jnp.dot(tri, d_a_hi, preferred_element_type=F32)
                + jnp.dot(tri, d_a_lo, preferred_element_type=F32))
        a_cs_t = a_cs.T
        a_last = a_cs[q - 1:q, :]
        dt_e = expand(dt)
        fac_e = expand(dt * jnp.exp2(a_last - a_cs))
        ea_e = expand(jnp.exp2(a_cs))
        xdt = xs * dt_e
        xfac = (xs * fac_e).astype(BF16)

        y_parts, cgs, d_states = [], [], []
        for g in range(SSD_GROUPS):
            bg = b_all[:, g * gw:(g + 1) * gw]
            cg = c_all[:, g * gw:(g + 1) * gw]
            cb = lax.dot_general(cg, bg, (((1,), (1,)), ((), ())),
                                 preferred_element_type=F32)
            bg_t = u[:, d_ssd + g * gw:d_ssd + (g + 1) * gw].T.astype(BF16)
            lo = g * group_w
            cgs.append(cg)
            d_states.append(jnp.dot(bg_t, xfac[:, lo:lo + group_w],
                                    preferred_element_type=F32))
            for m in range(heads_per_group // 2):
                base = lo + m * LANES
                x_pair = xdt[:, base:base + LANES]
                y_pair = None
                for h in range(2):
                    kh = base // HEAD_DIM + h
                    seg = a_cs[:, kh:kh + 1] - a_cs_t[kh:kh + 1, :]
                    decay = jnp.exp2(jnp.where(lower, seg, MASKED_LOG))
                    wts = (cb * decay).astype(BF16)
                    x_h = jnp.where(first if h == 0 else jnp.logical_not(first), x_pair, 0.0)
                    part = jnp.dot(wts, x_h.astype(BF16), preferred_element_type=F32)
                    y_pair = part if y_pair is None else y_pair + part
                y_parts.append(y_pair)
        y_local = jnp.concatenate(y_parts, axis=1) + dskip_ref[...] * xs
        return y_local, cgs, d_states, ea_e

    local = []
    u_next = conv_silu(0)
    for k in range(SSD_STEP_CHUNKS):
        u = u_next
        if k + 1 < SSD_STEP_CHUNKS:
            u_next = conv_silu(k + 1)
        local.append(chunk_local(k, u))

    states = [state_ref[:, g * group_w:(g + 1) * group_w] for g in range(SSD_GROUPS)]
    for k, (y_local, cgs, d_states, ea_e) in enumerate(local):
        y_off = jnp.concatenate(
            [jnp.dot(cgs[g], states[g].astype(BF16), preferred_element_type=F32)
             for g in range(SSD_GROUPS)], axis=1) * ea_e
        states = [states[g] * ea_e[q - 1:q, g * group_w:(g + 1) * group_w] + d_states[g]
                  for g in range(SSD_GROUPS)]
        rows = pl.ds(k * q, q)
        gated = (y_local + y_off) * g_ref[0, rows, :].astype(F32)
        ms = jnp.mean(gated * gated, axis=-1, keepdims=True)
        y_ref[0, rows, :] = (gated * lax.rsqrt(ms + EPS) * nw_ref[...]).astype(y_ref.dtype)
    for g in range(SSD_GROUPS):
        state_ref[:, g * group_w:(g + 1) * group_w] = states[g]


def _ssd(xbc, dt_raw, gate, conv_w, conv_b, dt_bias, a_log, d_skip, ssd_norm_w, d_ssd):
    b, l, conv_dim = xbc.shape
    heads = d_ssd // HEAD_DIM
    rows = SSD_CHUNK * SSD_STEP_CHUNKS
    assert l % rows == 0
    pad = LANES - heads
    dtb = jnp.pad(dt_bias.astype(F32), (0, pad)).reshape(1, LANES)
    alog = jnp.pad(a_log.astype(F32), (0, pad)).reshape(1, LANES)
    dskip_e = jnp.repeat(d_skip.astype(F32), HEAD_DIM).reshape(1, d_ssd)
    lane_head = lax.broadcasted_iota(jnp.int32, (LANES, d_ssd), 1) // HEAD_DIM
    row_head = lax.broadcasted_iota(jnp.int32, (LANES, d_ssd), 0)
    expand = jnp.where(lane_head == row_head, 1.0, 0.0).astype(BF16)
    chunk = lambda bi, ci: (bi, ci, 0)
    fixed = lambda bi, ci: (0, 0)
    return pl.pallas_call(
        functools.partial(_ssd_kernel, d_ssd=d_ssd),
        grid=(b, l // rows),
        in_specs=[
            pl.BlockSpec((1, rows, conv_dim), chunk),
            pl.BlockSpec((1, rows, LANES), chunk),
            pl.BlockSpec((1, rows, d_ssd), chunk),
            pl.BlockSpec((CONV_W, conv_dim), fixed),
            pl.BlockSpec((1, conv_dim), fixed),
            pl.BlockSpec((1, LANES), fixed),
            pl.BlockSpec((1, LANES), fixed),
            pl.BlockSpec((1, d_ssd), fixed),
            pl.BlockSpec((1, d_ssd), fixed),
            pl.BlockSpec((LANES, d_ssd), fixed),
        ],
        out_specs=pl.BlockSpec((1, rows, d_ssd), chunk),
        out_shape=jax.ShapeDtypeStruct((b, l, d_ssd), BF16),
        scratch_shapes=[pltpu.VMEM((rows + CONV_HALO, conv_dim), F32),
                        pltpu.VMEM((SSD_STATE, d_ssd), F32)],
        compiler_params=pltpu.CompilerParams(
            dimension_semantics=("arbitrary", "arbitrary"), vmem_limit_bytes=VMEM_LIMIT),
        name="ssd",
    )(xbc, dt_raw, gate, conv_w.astype(F32), conv_b.astype(F32).reshape(1, conv_dim),
      dtb, alog, dskip_e, ssd_norm_w.astype(F32).reshape(1, d_ssd), expand)


def _out_proj_kernel(o_ref, g_ref, y_ref, x_ref, nw_ref, w_ref, out_ref):
    d_sb = o_ref.shape[-1]
    acc = x_ref[...] + jnp.dot(y_ref[...], w_ref[d_sb:, :], preferred_element_type=F32)
    gated = o_ref[...].astype(F32) * g_ref[...].astype(F32)
    ms = jnp.mean(gated * gated, axis=-1, keepdims=True)
    y_sb = (gated * lax.rsqrt(ms + EPS) * nw_ref[...]).astype(BF16)
    out_ref[...] = acc + jnp.dot(y_sb, w_ref[:d_sb, :], preferred_element_type=F32)


def _out_proj(o_sb, gate_sb, y_ssd, x2, sb_norm_w, w_out):
    m, d = x2.shape
    d_sb = o_sb.shape[-1]
    d_ssd = y_ssd.shape[-1]
    row = lambda i: (i, 0)
    fixed = lambda i: (0, 0)
    return pl.pallas_call(
        _out_proj_kernel,
        grid=(m // OUT_ROW_TILE,),
        in_specs=[
            pl.BlockSpec((OUT_ROW_TILE, d_sb), row),
            pl.BlockSpec((OUT_ROW_TILE, d_sb), row),
            pl.BlockSpec((OUT_ROW_TILE, d_ssd), row),
            pl.BlockSpec((OUT_ROW_TILE, d), row),
            pl.BlockSpec((1, d_sb), fixed),
            pl.BlockSpec(w_out.shape, fixed, pipeline_mode=pl.Buffered(1)),
        ],
        out_specs=pl.BlockSpec((OUT_ROW_TILE, d), row),
        out_shape=jax.ShapeDtypeStruct((m, d), F32),
        compiler_params=pltpu.CompilerParams(
            dimension_semantics=("arbitrary",), vmem_limit_bytes=VMEM_LIMIT),
        name="out_proj",
    )(o_sb, gate_sb, y_ssd, x2, sb_norm_w.astype(F32).reshape(1, d_sb), w_out)


def _layer(x, norm_w, w_in, q_norm_w, k_norm_w, conv_w, conv_b, dt_bias, a_log, d_skip,
           sb_norm_w, ssd_norm_w, w_out):
    b, l, d = x.shape
    d_sb = sb_norm_w.shape[0]
    d_ssd = ssd_norm_w.shape[0]
    conv_dim = conv_w.shape[1]
    heads = dt_bias.shape[0]
    main = 4 * d_sb + d_ssd + conv_dim
    assert w_in.shape[1] == main + heads and heads <= LANES
    assert conv_dim == d_ssd + 2 * SSD_GROUPS * SSD_STATE and heads * HEAD_DIM == d_ssd
    m = b * l
    x2 = x.reshape(m, d)
    w_main = w_in[:, :main].astype(BF16)
    w_dt = jnp.pad(w_in[:, main:], ((0, 0), (0, LANES - heads))).astype(BF16)
    q, k, v, gate_sb, gate_ssd, xbc, dt_raw = _in_proj(
        x2, norm_w.astype(F32), w_main, w_dt, d_sb, d_ssd, conv_dim)
    o_sb = _sb_attention(q.reshape(b, l, d_sb), k.reshape(b, l, d_sb), v.reshape(b, l, d_sb),
                         q_norm_w, k_norm_w)
    y_ssd = _ssd(xbc.reshape(b, l, conv_dim), dt_raw.reshape(b, l, LANES),
                 gate_ssd.reshape(b, l, d_ssd), conv_w, conv_b, dt_bias, a_log, d_skip,
                 ssd_norm_w, d_ssd)
    out = _out_proj(o_sb.reshape(m, d_sb), gate_sb, y_ssd.reshape(m, d_ssd), x2, sb_norm_w,
                    w_out.astype(BF16))
    return out.reshape(b, l, d)


def kernel(x, norm_w, w_in, q_norm_w, k_norm_w, conv_w, conv_b, dt_bias, A_log, D_skip,
           sb_norm_w, ssd_norm_w, w_out):
    for layer in range(norm_w.shape[0]):
        x = _layer(x, norm_w[layer], w_in[layer], q_norm_w[layer], k_norm_w[layer],
                   conv_w[layer], conv_b[layer], dt_bias[layer], A_log[layer], D_skip[layer],
                   sb_norm_w[layer], ssd_norm_w[layer], w_out[layer])
    return x
```
